```python
import math
import jax, jax.numpy as jnp
from jax import lax
import numpy as np

D_MODEL = 2048
BATCH = 4
SEQ = 4096
DEPTH = 2

N_EVEN = (DEPTH + 1) // 2
N_ODD = DEPTH // 2
EPS = 1e-6

SSD_HEADS = 32
SSD_HEAD_DIM = 64
SSD_INNER = SSD_HEADS * SSD_HEAD_DIM
SSD_GROUPS = 8
SSD_STATE = 128
SSD_CONV = 4
SSD_CHUNK = 128
SSD_CONV_DIM = SSD_INNER + 2 * SSD_GROUPS * SSD_STATE

GMLP_GROUPS = 16
GMLP_GROUP_DIM = 128
GMLP_INNER = GMLP_GROUPS * GMLP_GROUP_DIM
GMLP_CHUNK = 128

IN0_DIM = SSD_INNER + SSD_CONV_DIM + SSD_HEADS + 2 * GMLP_INNER
MIX0_DIM = SSD_INNER + GMLP_INNER

SB_HEADS = 16
SB_HEAD_DIM = D_MODEL // SB_HEADS
SB_BLOCK = 128

PEER_HEADS = 8
PEER_KEYS = 128
PEER_EXPERTS = PEER_KEYS * PEER_KEYS
PEER_KEY_DIM = 256
PEER_HALF = PEER_KEY_DIM // 2
PEER_TOPK = 16
PEER_TOKEN_BLOCK = 128

kernel_name = 'hybrid_ssd_gmlp_stickbreak_peer'


def rms_norm(x, g):
    xf = x.astype(jnp.float32)
    y = xf * lax.rsqrt(jnp.mean(xf * xf, axis=-1, keepdims=True) + EPS)
    return (y * g.astype(jnp.float32)).astype(x.dtype)


def layer_norm(x, g, b):
    xf = x.astype(jnp.float32)
    mu = jnp.mean(xf, axis=-1, keepdims=True)
    xc = xf - mu
    var = jnp.mean(xc * xc, axis=-1, keepdims=True)
    y = xc * lax.rsqrt(var + EPS) * g.astype(jnp.float32) + b.astype(jnp.float32)
    return y.astype(x.dtype)


def modulate(h, shift, scale):
    return h * (1 + scale[:, None, :]) + shift[:, None, :]


def causal_depthwise_conv(x, w, b):
    K, C = w.shape
    y = lax.conv_general_dilated(
        x, w[:, None, :].astype(x.dtype), window_strides=(1,), padding=[(K - 1, 0)],
        dimension_numbers=('NWC', 'WIO', 'NWC'), feature_group_count=C)
    return y + b


def ssd_chunked_scan(xs, dt, A, Bm, Cm):
    f32 = jnp.float32
    Bsz, S, H, P = xs.shape
    G, N = Bm.shape[2], Bm.shape[3]
    R = H // G
    L = SSD_CHUNK
    nc = S // L
    x = (xs.astype(f32) * dt[..., None]).reshape(Bsz, nc, L, G, R, P)
    a = (dt * A).reshape(Bsz, nc, L, G, R).transpose(0, 3, 4, 1, 2)
    a_cum = jnp.cumsum(a, axis=-1)
    Bc = Bm.astype(f32).reshape(Bsz, nc, L, G, N)
    Cc = Cm.astype(f32).reshape(Bsz, nc, L, G, N)
    causal = jnp.tril(jnp.ones((L, L), dtype=bool))
    seg = a_cum[..., :, None] - a_cum[..., None, :]
    decay = jnp.where(causal, jnp.exp(jnp.where(causal, seg, 0.0)), 0.0)
    cb = jnp.einsum('bclgn,bcsgn->bcgls', Cc, Bc)
    y_diag = jnp.einsum('bcgls,bgrcls,bcsgrp->bclgrp', cb, decay, x)
    decay_to_end = jnp.exp(a_cum[..., -1:] - a_cum)
    states = jnp.einsum('bclgn,bgrcl,bclgrp->bcgrpn', Bc, decay_to_end, x)
    chunk_decay = jnp.exp(a_cum[..., -1])

    def step(h, inp):
        st, dec = inp
        return h * dec[..., None, None] + st, h

    h0 = jnp.zeros((Bsz, G, R, P, N), f32)
    _, h_in = lax.scan(step, h0, (states.transpose(1, 0, 2, 3, 4, 5),
                                  chunk_decay.transpose(3, 0, 1, 2)))
    y_off = jnp.einsum('bclgn,cbgrpn,bgrcl->bclgrp', Cc, h_in, jnp.exp(a_cum))
    return (y_diag + y_off).reshape(Bsz, S, H, P)


def ssd_gmlp_mixer(h, in_w, conv_w, conv_b, dt_bias, a_log, d_skip, ssd_norm_g,
                   gmlp_ln_g, gmlp_ln_b, gmlp_ws, gmlp_bs, out_w):
    Bsz, S, _ = h.shape
    proj = h @ in_w
    cuts = np.cumsum([SSD_INNER, SSD_CONV_DIM, SSD_HEADS, GMLP_INNER]).tolist()
    z, xbc, dt_raw, u, v = jnp.split(proj, cuts, axis=-1)
    xbc = jax.nn.silu(causal_depthwise_conv(xbc, conv_w, conv_b))
    xs, Bm, Cm = jnp.split(xbc, [SSD_INNER, SSD_INNER + SSD_GROUPS * SSD_STATE], axis=-1)
    xs = xs.reshape(Bsz, S, SSD_HEADS, SSD_HEAD_DIM)
    Bm = Bm.reshape(Bsz, S, SSD_GROUPS, SSD_STATE)
    Cm = Cm.reshape(Bsz, S, SSD_GROUPS, SSD_STATE)
    dt = jax.nn.softplus(dt_raw.astype(jnp.float32) + dt_bias.astype(jnp.float32))
    A = -jnp.exp(a_log.astype(jnp.float32))
    y = ssd_chunked_scan(xs, dt, A, Bm, Cm) + d_skip.astype(jnp.float32)[:, None] * xs.astype(jnp.float32)
    y = y.reshape(Bsz, S, SSD_INNER).astype(h.dtype)
    y_a = rms_norm(y * jax.nn.silu(z), ssd_norm_g)
    u = jax.nn.gelu(u, approximate=False)
    v = layer_norm(jax.nn.gelu(v, approximate=False), gmlp_ln_g, gmlp_ln_b)
    v = v.reshape(Bsz, S // GMLP_CHUNK, GMLP_CHUNK, GMLP_GROUPS, GMLP_GROUP_DIM)
    ws = gmlp_ws * jnp.tril(jnp.ones((GMLP_CHUNK, GMLP_CHUNK), gmlp_ws.dtype))
    v = jnp.einsum('gts,bnsgc->bntgc', ws, v) + gmlp_bs.T[:, :, None]
    y_b = u * v.reshape(Bsz, S, GMLP_INNER)
    return jnp.concatenate([y_a, y_b], axis=-1) @ out_w


def stick_breaking_attention(h, qkv_w, out_w):
    Bsz, S, _ = h.shape
    qkv = (h @ qkv_w).reshape(Bsz, S, 3, SB_HEADS, SB_HEAD_DIM)
    q = qkv[:, :, 0].transpose(0, 2, 1, 3)
    k = qkv[:, :, 1].transpose(0, 2, 1, 3)
    v = qkv[:, :, 2].transpose(0, 2, 1, 3)
    nb = S // SB_BLOCK
    q_blocks = q.reshape(Bsz, SB_HEADS, nb, SB_BLOCK, SB_HEAD_DIM).transpose(2, 0, 1, 3, 4)
    key_pos = jnp.arange(S)
    scale = SB_HEAD_DIM ** -0.5

    def block(args):
        qb, start = args
        logits = jnp.einsum('bhqd,bhkd->bhqk', qb, k).astype(jnp.float32) * scale
        q_pos = start + jnp.arange(SB_BLOCK)
        mask = key_pos[None, :] < q_pos[:, None]
        log_beta = jax.nn.log_sigmoid(logits)
        log_1m_beta = jnp.where(mask, jax.nn.log_sigmoid(-logits), 0.0)
        tail = lax.cumsum(log_1m_beta, axis=3, reverse=True) - log_1m_beta
        weights = jnp.where(mask, jnp.exp(log_beta + tail), 0.0)
        return jnp.einsum('bhqk,bhkd->bhqd', weights.astype(v.dtype), v)

    starts = jnp.arange(nb, dtype=jnp.int32) * SB_BLOCK
    o = lax.map(block, (q_blocks, starts))
    o = o.transpose(1, 0, 3, 2, 4).reshape(Bsz, S, SB_HEADS * SB_HEAD_DIM)
    return o @ out_w


def peer_ffn(h, w_query, sub_keys, expert_u, expert_v):
    Bsz, S, D = h.shape
    T = Bsz * S
    hf = h.reshape(T, D)
    q = (hf @ w_query).reshape(T, PEER_HEADS, 2, PEER_HALF)
    scores = jnp.einsum('thic,hikc->thik', q, sub_keys).astype(jnp.float32)
    s, idx = lax.top_k(scores, PEER_TOPK)
    cand = s[:, :, 0, :, None] + s[:, :, 1, None, :]
    cand_idx = idx[:, :, 0, :, None] * PEER_KEYS + idx[:, :, 1, None, :]
    cand = cand.reshape(T, PEER_HEADS, PEER_TOPK * PEER_TOPK)
    cand_idx = cand_idx.reshape(T, PEER_HEADS, PEER_TOPK * PEER_TOPK)
    top_s, pos = lax.top_k(cand, PEER_TOPK)
    expert_idx = jnp.take_along_axis(cand_idx, pos, axis=-1)
    gates = jax.nn.softmax(top_s, axis=-1).astype(h.dtype)
    nblk = T // PEER_TOKEN_BLOCK
    n_sel = PEER_HEADS * PEER_TOPK

    def block(args):
        xb, eb, gb = args
        u = jnp.take(expert_u, eb, axis=0)
        act = jax.nn.gelu(jnp.einsum('tnd,td->tn', u, xb), approximate=False) * gb
        vv = jnp.take(expert_v, eb, axis=0)
        return jnp.einsum('tn,tnd->td', act, vv)

    out = lax.map(block, (hf.reshape(nblk, PEER_TOKEN_BLOCK, D),
                          expert_idx.reshape(nblk, PEER_TOKEN_BLOCK, n_sel),
                          gates.reshape(nblk, PEER_TOKEN_BLOCK, n_sel)))
    return out.reshape(Bsz, S, D)


def setup_inputs(seed: int = 0) -> dict:
    key = jax.random.key(seed)
    ks = jax.random.split(key, 26)
    f32 = jnp.float32
    D = D_MODEL

    def nrm(k, shape, scale):
        return jax.random.normal(k, shape, f32) * scale

    dt0 = jnp.exp(jax.random.uniform(ks[9], (N_EVEN, SSD_HEADS), f32,
                                     minval=math.log(1e-3), maxval=math.log(1e-1)))
    return {
        'x': nrm(ks[0], (BATCH, SEQ, D), 1.0),
        'c': nrm(ks[1], (BATCH, D), 1.0),
        'ada_w': nrm(ks[2], (DEPTH, D, 6 * D), 0.5 * D ** -0.5),
        'ada_b': nrm(ks[3], (DEPTH, 6 * D), 0.01),
        'norm_mix_g': 1.0 + nrm(ks[4], (DEPTH, D), 0.02),
        'norm_ffn_g': 1.0 + nrm(ks[5], (DEPTH, D), 0.02),
        'in0_w': nrm(ks[6], (N_EVEN, D, IN0_DIM), D ** -0.5),
        'conv_w': nrm(ks[7], (N_EVEN, SSD_CONV, SSD_CONV_DIM), SSD_CONV ** -0.5),
        'conv_b': nrm(ks[8], (N_EVEN, SSD_CONV_DIM), 0.01),
        'dt_bias': dt0 + jnp.log(-jnp.expm1(-dt0)),
        'a_log': jnp.log(jax.random.uniform(ks[10], (N_EVEN, SSD_HEADS), f32, minval=1.0, maxval=16.0)),
        'd_skip': 1.0 + nrm(ks[11], (N_EVEN, SSD_HEADS), 0.02),
        'ssd_norm_g': 1.0 + nrm(ks[12], (N_EVEN, SSD_INNER), 0.02),
        'gmlp_ln_g': 1.0 + nrm(ks[13], (N_EVEN, GMLP_INNER), 0.02),
        'gmlp_ln_b': nrm(ks[14], (N_EVEN, GMLP_INNER), 0.01),
        'gmlp_ws': nrm(ks[15], (N_EVEN, GMLP_GROUPS, GMLP_CHUNK, GMLP_CHUNK), GMLP_CHUNK ** -0.5),
        'gmlp_bs': 1.0 + nrm(ks[16], (N_EVEN, GMLP_GROUPS, GMLP_CHUNK), 0.02),
        'out0_w': nrm(ks[17], (N_EVEN, MIX0_DIM, D), MIX0_DIM ** -0.5),
        'sb_qkv_w': nrm(ks[18], (N_ODD, D, 3 * D), D ** -0.5),
        'sb_out_w': nrm(ks[19], (N_ODD, D, D), D ** -0.5),
        'peer_wq': nrm(ks[20], (DEPTH, D, PEER_HEADS * PEER_KEY_DIM), D ** -0.5),
        'peer_keys': nrm(ks[21], (DEPTH, PEER_HEADS, 2, PEER_KEYS, PEER_HALF), PEER_HALF ** -0.5),
        'peer_u': nrm(ks[22], (DEPTH, PEER_EXPERTS, D), D ** -0.5),
        'peer_v': nrm(ks[23], (DEPTH, PEER_EXPERTS, D), PEER_HEADS ** -0.5),
        'final_g': 1.0 + nrm(ks[24], (D,), 0.02),
    }


def reference(x, c, ada_w, ada_b, norm_mix_g, norm_ffn_g, in0_w, conv_w, conv_b,
              dt_bias, a_log, d_skip, ssd_norm_g, gmlp_ln_g, gmlp_ln_b, gmlp_ws,
              gmlp_bs, out0_w, sb_qkv_w, sb_out_w, peer_wq, peer_keys, peer_u,
              peer_v, final_g):
    cond = jax.nn.silu(c)
    for i in range(DEPTH):
        mod = cond @ ada_w[i] + ada_b[i]
        shift1, scale1, gate1, shift2, scale2, gate2 = jnp.split(mod, 6, axis=-1)
        h = modulate(rms_norm(x, norm_mix_g[i]), shift1, scale1)
        j = i // 2
        if i % 2 == 0:
            mix = ssd_gmlp_mixer(h, in0_w[j], conv_w[j], conv_b[j], dt_bias[j], a_log[j],
                                 d_skip[j], ssd_norm_g[j], gmlp_ln_g[j], gmlp_ln_b[j],
                                 gmlp_ws[j], gmlp_bs[j], out0_w[j])
        else:
            mix = stick_breaking_attention(h, sb_qkv_w[j], sb_out_w[j])
        x = x + gate1[:, None, :] * mix
        h = modulate(rms_norm(x, norm_ffn_g[i]), shift2, scale2)
        x = x + gate2[:, None, :] * peer_ffn(h, peer_wq[i], peer_keys[i], peer_u[i], peer_v[i])
    return rms_norm(x, final_g)
```

```python
import functools

import numpy as np
import jax
import jax.numpy as jnp
from jax import lax
from jax.experimental import pallas as pl
from jax.experimental.pallas import tpu as pltpu

F32 = jnp.float32
BF16 = jnp.bfloat16
EPS = 1e-6
NEG = -1e30
INV_SQRT2 = 0.7071067811865476

SSD_HEADS = 32
SSD_HEAD_DIM = 64
SSD_GROUPS = 8
SSD_STATE = 128
SSD_CONV = 4
SSD_CHUNK = 128
GMLP_GROUPS = 16
GMLP_GROUP_DIM = 128
GMLP_CHUNK = 128
SB_HEADS = 16
SB_HEAD_DIM = 128
PEER_HEADS = 8
PEER_KEYS = 128
PEER_HALF = 128
PEER_TOPK = 16

LANES = 128
SUBLANES = 8
VMEM_LIMIT = 56 * 1024 * 1024


def _params(*sem, vmem=VMEM_LIMIT):
    return pltpu.CompilerParams(dimension_semantics=sem, vmem_limit_bytes=vmem)


def _gelu(x):
    return 0.5 * x * (1.0 + lax.erf(x * INV_SQRT2))


def _silu(x):
    return x * jax.nn.sigmoid(x)


def _softplus(x):
    return jnp.maximum(x, 0.0) + jnp.log1p(jnp.exp(-jnp.abs(x)))


def _split2(v):
    hi = v.astype(BF16)
    lo = (v - hi.astype(F32)).astype(BF16)
    return hi, lo


def _split3(v):
    hi = v.astype(BF16)
    r = v - hi.astype(F32)
    mid = r.astype(BF16)
    lo = (r - mid.astype(F32)).astype(BF16)
    return hi, mid, lo


def _ada_kernel(c_ref, w_ref, b_ref, o_ref):
    cond = _silu(c_ref[...]).astype(BF16)
    o_ref[0] = jnp.dot(cond, w_ref[0].astype(BF16), preferred_element_type=F32) + b_ref[0]


def _ada_mod(c, ada_w, ada_b, tn=512):
    depth, d, n = ada_w.shape
    bsz = c.shape[0]
    rows = -(-bsz // SUBLANES) * SUBLANES
    c_pad = jnp.pad(c, ((0, rows - bsz), (0, 0)))
    out = pl.pallas_call(
        _ada_kernel,
        grid=(depth, n // tn),
        in_specs=[pl.BlockSpec((rows, d), lambda i, j: (0, 0)),
                  pl.BlockSpec((1, d, tn), lambda i, j: (i, 0, j)),
                  pl.BlockSpec((1, 1, tn), lambda i, j: (i, 0, j))],
        out_specs=pl.BlockSpec((1, rows, tn), lambda i, j: (i, 0, j)),
        out_shape=jax.ShapeDtypeStruct((depth, rows, n), F32),
        compiler_params=_params("arbitrary", "arbitrary"),
        name="ada_mod",
    )(c_pad, ada_w, ada_b.reshape(depth, 1, n))
    return out[:, :bsz]


def _normmod_kernel(x_ref, g_ref, sh_ref, sc_ref, o_ref, *, transposed):
    x = x_ref[...]
    y = x * lax.rsqrt(jnp.mean(x * x, axis=-1, keepdims=True) + EPS) * g_ref[...]
    h = y * (1.0 + sc_ref[0]) + sh_ref[0]
    o_ref[...] = (h.T if transposed else h).astype(BF16)


def _normmod(x2, g, shift, scale, seq, transposed, tm=256):
    t, d = x2.shape
    per_batch = seq // tm
    if transposed:
        out_spec = pl.BlockSpec((d, tm), lambda i: (0, i))
        out_shape = jax.ShapeDtypeStruct((d, t), BF16)
    else:
        out_spec = pl.BlockSpec((tm, d), lambda i: (i, 0))
        out_shape = jax.ShapeDtypeStruct((t, d), BF16)
    return pl.pallas_call(
        functools.partial(_normmod_kernel, transposed=transposed),
        grid=(t // tm,),
        in_specs=[pl.BlockSpec((tm, d), lambda i: (i, 0)),
                  pl.BlockSpec((1, d), lambda i: (0, 0)),
                  pl.BlockSpec((1, 1, d), lambda i: (i // per_batch, 0, 0)),
                  pl.BlockSpec((1, 1, d), lambda i: (i // per_batch, 0, 0))],
        out_specs=out_spec,
        out_shape=out_shape,
        compiler_params=_params("arbitrary"),
        name="normmod_t" if transposed else "normmod",
    )(x2, g.reshape(1, d), shift, scale)


def _final_norm_kernel(x_ref, g_ref, o_ref):
    x = x_ref[...]
    o_ref[...] = x * lax.rsqrt(jnp.mean(x * x, axis=-1, keepdims=True) + EPS) * g_ref[...]


def _final_norm(x2, g, tm=256):
    t, d = x2.shape
    return pl.pallas_call(
        _final_norm_kernel,
        grid=(t // tm,),
        in_specs=[pl.BlockSpec((tm, d), lambda i: (i, 0)),
                  pl.BlockSpec((1, d), lambda i: (0, 0))],
        out_specs=pl.BlockSpec((tm, d), lambda i: (i, 0)),
        out_shape=jax.ShapeDtypeStruct((t, d), F32),
        compiler_params=_params("arbitrary"),
        name="final_norm",
    )(x2, g.reshape(1, d))


def _mm_kernel(*refs, n_pairs, has_resid):
    acc = None
    for a_ref, b_ref in zip(refs[:n_pairs], refs[n_pairs:2 * n_pairs]):
        part = jnp.dot(a_ref[...], b_ref[...], preferred_element_type=F32)
        acc = part if acc is None else acc + part
    if has_resid:
        x_ref, gate_ref, o_ref = refs[2 * n_pairs:]
        o_ref[...] = x_ref[...] + gate_ref[0] * acc
    else:
        o_ref = refs[2 * n_pairs]
        o_ref[...] = acc.astype(o_ref.dtype)


def _matmul(a_list, b_list, out_dtype, tm, tn, resid=None, gate=None, seq=None, name="matmul"):
    m = a_list[0].shape[0]
    n = b_list[0].shape[1]
    tm, tn = min(tm, m if seq is None else seq), min(tn, n)
    in_specs = [pl.BlockSpec((tm, a.shape[1]), lambda i, j: (i, 0)) for a in a_list]
    in_specs += [pl.BlockSpec((b.shape[0], tn), lambda i, j: (0, j)) for b in b_list]
    args = list(a_list) + list(b_list)
    if resid is not None:
        per_batch = seq // tm
        in_specs += [pl.BlockSpec((tm, tn), lambda i, j: (i, j)),
                     pl.BlockSpec((1, 1, tn), lambda i, j: (i // per_batch, 0, j))]
        args += [resid, gate]
    return pl.pallas_call(
        functools.partial(_mm_kernel, n_pairs=len(a_list), has_resid=resid is not None),
        grid=(m // tm, n // tn),
        in_specs=in_specs,
        out_specs=pl.BlockSpec((tm, tn), lambda i, j: (i, j)),
        out_shape=jax.ShapeDtypeStruct((m, n), out_dtype),
        compiler_params=_params("arbitrary", "arbitrary"),
        name=name,
    )(*args)


def _ssd_kernel(z_ref, xs_ref, bc_ref, dt_ref, cw_ref, cb_ref, dtb_ref, alog_ref, dskip_ref, ng_ref,
                expand_ref, o_ref, xbuf, bcbuf, ybuf, state_ref):
    L, H, P, G, N = SSD_CHUNK, SSD_HEADS, SSD_HEAD_DIM, SSD_GROUPS, SSD_STATE
    R = H // G
    XI = H * P
    HALO = SUBLANES

    @pl.when(pl.program_id(1) == 0)
    def _():
        xbuf[0:HALO, :] = jnp.zeros((HALO, XI), F32)
        bcbuf[0:HALO, :] = jnp.zeros((HALO, 2 * G * N), F32)
        state_ref[...] = jnp.zeros(state_ref.shape, F32)

    xbuf[HALO:HALO + L, :] = xs_ref[...].astype(F32)
    bcbuf[HALO:HALO + L, :] = bc_ref[...].astype(F32)

    def conv_silu(buf, w, b):
        acc = b
        for k in range(SSD_CONV):
            off = HALO - (SSD_CONV - 1) + k
            acc = acc + w[k:k + 1, :] * buf[off:off + L, :]
        return _silu(acc)

    cw = cw_ref[...]
    cb = cb_ref[...]
    xs = conv_silu(xbuf, cw[:, :XI], cb[:, :XI])
    bc = conv_silu(bcbuf, cw[:, XI:], cb[:, XI:])
    xbuf[0:HALO, :] = xbuf[L:L + HALO, :]
    bcbuf[0:HALO, :] = bcbuf[L:L + HALO, :]

    dt = _softplus(dt_ref[...] + dtb_ref[...])
    a = dt * (-jnp.exp(alog_ref[...]))
    row = lax.broadcasted_iota(jnp.int32, (L, L), 0)
    col = lax.broadcasted_iota(jnp.int32, (L, L), 1)
    causal = col <= row
    tri = jnp.where(causal, 1.0, 0.0).astype(BF16)
    a_cum = sum(jnp.dot(tri, part, preferred_element_type=F32) for part in _split3(a))
    a_cum_t = a_cum.T
    a_last = a_cum[L - 1:L, :]
    dte = jnp.exp(a_last - a_cum)
    eac = jnp.exp(a_cum)

    stk = jnp.concatenate([dt, dt * dte, eac], axis=0)
    hi, lo = _split2(stk)
    ex = jnp.dot(jnp.concatenate([hi, lo], axis=1), expand_ref[...], preferred_element_type=F32)
    dt_e, dd_e, eac_e = ex[0:L], ex[L:2 * L], ex[2 * L:3 * L]
    x_in = xs * dt_e
    xd = (xs * dd_e).astype(BF16)
    cdec_e = eac_e[L - 1:L, :]

    GW = R * P
    lane_head = lax.broadcasted_iota(jnp.int32, (L, GW), 1) // P
    for g in range(G):
        lanes = slice(g * GW, (g + 1) * GW)
        bg = bc[:, g * N:(g + 1) * N]
        cg = bc[:, G * N + g * N:G * N + (g + 1) * N].astype(BF16)
        cbm = lax.dot_general(cg, bg.astype(BF16), (((1,), (1,)), ((), ())), preferred_element_type=F32)
        xg = x_in[:, lanes]
        ms, xblocks = [], []
        for r in range(R):
            h = g * R + r
            seg = a_cum[:, h:h + 1] - a_cum_t[h:h + 1, :]
            dec = jnp.where(causal, jnp.exp(jnp.where(causal, seg, 0.0)), 0.0)
            ms.append((cbm * dec).astype(BF16))
            xblocks.append(jnp.where(lane_head == r, xg, 0.0).astype(BF16))
        y_diag = jnp.dot(jnp.concatenate(ms, axis=1), jnp.concatenate(xblocks, axis=0),
                         preferred_element_type=F32)
        hg = state_ref[g]
        y_off = jnp.dot(cg, hg.astype(BF16), preferred_element_type=F32) * eac_e[:, lanes]
        new_states = jnp.dot(bg.T.astype(BF16), xd[:, lanes], preferred_element_type=F32)
        state_ref[g] = hg * cdec_e[:, lanes] + new_states
        ybuf[:, lanes] = y_diag + y_off + dskip_ref[:, lanes] * xs[:, lanes]

    z = z_ref[...].astype(F32)
    yz = ybuf[...] * _silu(z)
    ya = yz * lax.rsqrt(jnp.mean(yz * yz, axis=-1, keepdims=True) + EPS) * ng_ref[...]
    o_ref[...] = ya.astype(BF16)


def _ssd(proj, dt_raw, conv_w, conv_b, dt_bias, a_log, d_skip, norm_g, seq):
    t = proj.shape[0]
    L, H, P, G, N = SSD_CHUNK, SSD_HEADS, SSD_HEAD_DIM, SSD_GROUPS, SSD_STATE
    xi = H * P
    bcw = 2 * G * N
    nc = seq // L
    pad = LANES - H
    expand = np.zeros((LANES, xi), np.float32)
    expand[np.arange(xi) // P, np.arange(xi)] = 1.0
    expand2 = jnp.asarray(np.concatenate([expand, expand], axis=0), BF16)
    row = lambda v: jnp.pad(v.astype(F32), (0, pad)).reshape(1, LANES)
    full = lambda shape: pl.BlockSpec(shape, lambda b, c: (0, 0))
    return pl.pallas_call(
        _ssd_kernel,
        grid=(t // seq, nc),
        in_specs=[pl.BlockSpec((L, xi), lambda b, c: (b * nc + c, 0)),
                  pl.BlockSpec((L, xi), lambda b, c: (b * nc + c, 1)),
                  pl.BlockSpec((L, bcw), lambda b, c: (b * nc + c, 2)),
                  pl.BlockSpec((L, LANES), lambda b, c: (b * nc + c, 0)),
                  full((SSD_CONV, xi + bcw)), full((1, xi + bcw)),
                  full((1, LANES)), full((1, LANES)), full((1, xi)), full((1, xi)),
                  full((2 * LANES, xi))],
        out_specs=pl.BlockSpec((L, xi), lambda b, c: (b * nc + c, 0)),
        out_shape=jax.ShapeDtypeStruct((t, xi), BF16),
        scratch_shapes=[pltpu.VMEM((SUBLANES + L, xi), F32),
                        pltpu.VMEM((SUBLANES + L, bcw), F32),
                        pltpu.VMEM((L, xi), F32),
                        pltpu.VMEM((G, N, (H // G) * P), F32)],
        compiler_params=_params("arbitrary", "arbitrary"),
        name="ssd",
    )(proj, proj, proj, dt_raw, conv_w, conv_b.reshape(1, -1), row(dt_bias), row(a_log),
      jnp.repeat(d_skip.astype(F32), P).reshape(1, xi), norm_g.reshape(1, xi), expand2)


def _gmlp_kernel(u_ref, v_ref, lng_ref, lnb_ref, ws_ref, bs_ref, o_ref, *, nchunk):
    L, NG, GD = GMLP_CHUNK, GMLP_GROUPS, GMLP_GROUP_DIM
    row = lax.broadcasted_iota(jnp.int32, (L, L), 0)
    col = lax.broadcasted_iota(jnp.int32, (L, L), 1)
    tril = col <= row
    for ch in range(nchunk):
        rows = slice(ch * L, (ch + 1) * L)
        ug = _gelu(u_ref[rows, :].astype(F32))
        vg = _gelu(v_ref[rows, :].astype(F32))
        mu = jnp.mean(vg, axis=-1, keepdims=True)
        xc = vg - mu
        var = jnp.mean(xc * xc, axis=-1, keepdims=True)
        vn = (xc * lax.rsqrt(var + EPS) * lng_ref[...] + lnb_ref[...]).astype(BF16)
        for g in range(NG):
            cols = slice(g * GD, (g + 1) * GD)
            w = jnp.where(tril, ws_ref[g], 0.0).astype(BF16)
            vv = jnp.dot(w, vn[:, cols], preferred_element_type=F32) + bs_ref[:, cols]
            o_ref[rows, cols] = (ug[:, cols] * vv).astype(BF16)


def _gmlp(proj, ln_g, ln_b, ws, bs, nchunk=2):
    t = proj.shape[0]
    L, NG, GD = GMLP_CHUNK, GMLP_GROUPS, GMLP_GROUP_DIM
    w = NG * GD
    tm = nchunk * L
    bs_full = jnp.repeat(bs.T.astype(F32), GD, axis=1)
    return pl.pallas_call(
        functools.partial(_gmlp_kernel, nchunk=nchunk),
        grid=(t // tm,),
        in_specs=[pl.BlockSpec((tm, w), lambda i: (i, 3)),
                  pl.BlockSpec((tm, w), lambda i: (i, 4)),
                  pl.BlockSpec((1, w), lambda i: (0, 0)),
                  pl.BlockSpec((1, w), lambda i: (0, 0)),
                  pl.BlockSpec((NG, L, L), lambda i: (0, 0, 0)),
                  pl.BlockSpec((L, w), lambda i: (0, 0))],
        out_specs=pl.BlockSpec((tm, w), lambda i: (i, 0)),
        out_shape=jax.ShapeDtypeStruct((t, w), BF16),
        compiler_params=_params("arbitrary"),
        name="gmlp",
    )(proj, proj, ln_g.reshape(1, w), ln_b.reshape(1, w), ws, bs_full)


def _attn_kernel(q_ref, k_ref, v_ref, o_ref, *, tq):
    i = pl.program_id(2)
    scale = SB_HEAD_DIM ** -0.5
    q = q_ref[...]
    row = lax.broadcasted_iota(jnp.int32, (tq, tq), 0)
    col = lax.broadcasted_iota(jnp.int32, (tq, tq), 1)
    strict = col < row
    after1 = jnp.where(strict, 1.0, 0.0).astype(BF16)
    after = jnp.concatenate([after1, after1], axis=0)

    def tile(j, acc, run, diag):
        start = pl.multiple_of(j * tq, tq)
        kb = k_ref[pl.ds(start, tq), :]
        vb = v_ref[pl.ds(start, tq), :]
        logits = lax.dot_general(q, kb, (((1,), (1,)), ((), ())), preferred_element_type=F32) * scale
        sp = jnp.log1p(jnp.exp(-jnp.abs(logits)))
        log_beta = jnp.minimum(logits, 0.0) - sp
        log_1m = log_beta - logits
        if diag:
            log_1m = jnp.where(strict, log_1m, 0.0)
        hi, lo = _split2(log_1m)
        tail = jnp.dot(jnp.concatenate([hi, lo], axis=1), after, preferred_element_type=F32)
        w = jnp.exp(log_beta + tail + run)
        if diag:
            w = jnp.where(strict, w, 0.0)
        acc = acc + jnp.dot(w.astype(BF16), vb, preferred_element_type=F32)
        run = run + jnp.sum(log_1m, axis=-1, keepdims=True)
        return acc, run

    acc, run = tile(i, jnp.zeros((tq, SB_HEAD_DIM), F32), jnp.zeros((tq, 1), F32), True)

    def body(s, carry):
        return tile(i - 1 - s, carry[0], carry[1], False)

    acc, run = lax.fori_loop(0, i, body, (acc, run))
    o_ref[...] = acc.astype(BF16)


def _attn(qkv, seq, tq=256):
    t = qkv.shape[0]
    hd, nh = SB_HEAD_DIM, SB_HEADS
    nq = seq // tq
    return pl.pallas_call(
        functools.partial(_attn_kernel, tq=tq),
        grid=(t // seq, nh, nq),
        in_specs=[pl.BlockSpec((tq, hd), lambda b, h, i: (b * nq + i, h)),
                  pl.BlockSpec((seq, hd), lambda b, h, i: (b, nh + h)),
                  pl.BlockSpec((seq, hd), lambda b, h, i: (b, 2 * nh + h))],
        out_specs=pl.BlockSpec((tq, hd), lambda b, h, i: (b * nq + i, h)),
        out_shape=jax.ShapeDtypeStruct((t, nh * hd), BF16),
        compiler_params=_params("arbitrary", "arbitrary", "arbitrary"),
        name="sb_attn",
    )(qkv, qkv, qkv)


def _extract_top(work, count):
    tops = []
    for k in range(count):
        m = jnp.max(work, axis=0, keepdims=True)
        tops.append(m)
        if k + 1 < count:
            work = jnp.where(work == m, NEG, work)
    return tops


def _peer_topk_kernel(ht_ref, wqt_ref, keys_ref, d0_ref, p0_ref, s1_ref, p1z_ref, q_scr, cand_scr):
    nh, kd, topk = PEER_HEADS, PEER_HALF, PEER_TOPK
    n_cand = sum((topk + 1) // (k + 1) for k in range(topk + 1))
    q_scr[...] = jnp.dot(wqt_ref[...], ht_ref[...], preferred_element_type=F32).astype(BF16)
    cand_scr[...] = jnp.full(cand_scr.shape, NEG, F32)

    def head(h, carry):
        scores, tops = [], []
        for half in range(2):
            start = pl.multiple_of(h * (2 * kd) + half * kd, kd)
            sc = jnp.dot(keys_ref[h, half], q_scr[pl.ds(start, kd), :], preferred_element_type=F32)
            scores.append(sc)
            tops.append(_extract_top(sc, topk + 1))
        r = 0
        for k in range(topk + 1):
            for l in range((topk + 1) // (k + 1)):
                cand_scr[pl.ds(r, 1), :] = tops[0][k] + tops[1][l]
                r += 1
        cand = cand_scr[...]
        best = _extract_top(cand, topk + 1)
        theta = 0.5 * (best[topk - 1] + best[topk])
        zsum = jnp.sum(jnp.where(cand > theta, jnp.exp(cand - best[0]), 0.0), axis=0, keepdims=True)
        d0_ref[h] = theta - scores[0]
        p0_ref[h] = jnp.exp(scores[0] - tops[0][0])
        s1_ref[h] = scores[1]
        p1z_ref[h] = jnp.exp(scores[1] - tops[1][0]) / zsum
        return carry

    assert n_cand <= cand_scr.shape[0]
    lax.fori_loop(0, nh, head, 0)


def _peer_topk(ht, wqt, keys, tb=256):
    d, t = ht.shape
    nh, nk, kd = PEER_HEADS, PEER_KEYS, PEER_HALF
    n_cand = sum((PEER_TOPK + 1) // (k + 1) for k in range(PEER_TOPK + 1))
    cand_rows = -(-n_cand // SUBLANES) * SUBLANES
    out_spec = pl.BlockSpec((nh, nk, tb), lambda i: (0, 0, i))
    out_shape = jax.ShapeDtypeStruct((nh, nk, t), F32)
    return pl.pallas_call(
        _peer_topk_kernel,
        grid=(t // tb,),
        in_specs=[pl.BlockSpec((d, tb), lambda i: (0, i)),
                  pl.BlockSpec((nh * 2 * kd, d), lambda i: (0, 0)),
                  pl.BlockSpec((nh, 2, nk, kd), lambda i: (0, 0, 0, 0))],
        out_specs=[out_spec] * 4,
        out_shape=[out_shape] * 4,
        scratch_shapes=[pltpu.VMEM((nh * 2 * kd, tb), BF16),
                        pltpu.VMEM((cand_rows, tb), F32)],
        compiler_params=_params("arbitrary"),
        name="peer_topk",
    )(ht, wqt, keys)


def _peer_dense_kernel(ht_ref, u_ref, vt_ref, d0_ref, p0_ref, s1_ref, p1z_ref, x_ref, gate_ref, o_ref,
                       acc_ref, a_scr, g_scr, *, ni, tb):
    nh, nk = PEER_HEADS, PEER_KEYS
    e = pl.program_id(1)

    @pl.when(e == 0)
    def _():
        acc_ref[...] = jnp.zeros(acc_ref.shape, F32)

    a_scr[...] = jnp.dot(u_ref[...], ht_ref[...], preferred_element_type=F32)

    def body(ii, carry):
        r0 = pl.multiple_of(ii * nk, nk)
        for cc in range(tb // LANES):
            cs = slice(cc * LANES, (cc + 1) * LANES)
            w = jnp.zeros((nk, LANES), F32)
            for h in range(nh):
                thr = d0_ref[ii, h:h + 1, cs]
                p0 = p0_ref[ii, h:h + 1, cs]
                w = w + jnp.where(s1_ref[h, :, cs] >= thr, p1z_ref[h, :, cs], 0.0) * p0
            g_scr[pl.ds(r0, nk), cs] = (_gelu(a_scr[pl.ds(r0, nk), cs]) * w).astype(BF16)
        return carry

    lax.fori_loop(0, ni, body, 0)
    acc_ref[...] += jnp.dot(vt_ref[...], g_scr[...], preferred_element_type=F32)

    @pl.when(e == pl.num_programs(1) - 1)
    def _():
        o_ref[...] = x_ref[...] + gate_ref[0] * acc_ref[...].T


def _peer_dense(ht, u, vt, d0, p0, s1, p1z, x2, gate, seq, tb=512, eb=512):
    d, t = ht.shape
    n_exp = u.shape[0]
    nh, nk = PEER_HEADS, PEER_KEYS
    tb, eb = min(tb, seq), min(eb, n_exp)
    ni = eb // nk
    per_batch = seq // tb
    score_spec = pl.BlockSpec((nh, nk, tb), lambda i, e: (0, 0, i))
    first_spec = pl.BlockSpec((ni, nh, tb), lambda i, e: (e, 0, i))
    return pl.pallas_call(
        functools.partial(_peer_dense_kernel, ni=ni, tb=tb),
        grid=(t // tb, n_exp // eb),
        in_specs=[pl.BlockSpec((d, tb), lambda i, e: (0, i)),
                  pl.BlockSpec((eb, d), lambda i, e: (e, 0)),
                  pl.BlockSpec((d, eb), lambda i, e: (0, e)),
                  first_spec, first_spec, score_spec, score_spec,
                  pl.BlockSpec((tb, d), lambda i, e: (i, 0), pipeline_mode=pl.Buffered(1)),
                  pl.BlockSpec((1, 1, d), lambda i, e: (i // per_batch, 0, 0))],
        out_specs=pl.BlockSpec((tb, d), lambda i, e: (i, 0)),
        out_shape=jax.ShapeDtypeStruct((t, d), F32),
        scratch_shapes=[pltpu.VMEM((d, tb), F32),
                        pltpu.VMEM((eb, tb), F32),
                        pltpu.VMEM((eb, tb), BF16)],
        compiler_params=_params("arbitrary", "arbitrary"),
        name="peer_dense",
    )(ht, u, vt, d0.transpose(1, 0, 2), p0.transpose(1, 0, 2), s1, p1z, x2, gate)


def _peer(x2, g, shift, scale, gate, wq, keys, u, v, seq):
    ht = _normmod(x2, g, shift, scale, seq, transposed=True)
    d0, p0, s1, p1z = _peer_topk(ht, wq.T.astype(BF16), keys.astype(BF16))
    return _peer_dense(ht, u.astype(BF16), v.T.astype(BF16), d0, p0, s1, p1z, x2, gate, seq)


def kernel(x, c, ada_w, ada_b, norm_mix_g, norm_ffn_g, in0_w, conv_w, conv_b, dt_bias, a_log, d_skip,
           ssd_norm_g, gmlp_ln_g, gmlp_ln_b, gmlp_ws, gmlp_bs, out0_w, sb_qkv_w, sb_out_w, peer_wq,
           peer_keys, peer_u, peer_v, final_g):
    bsz, seq, d = x.shape
    depth = ada_w.shape[0]
    xi = SSD_HEADS * SSD_HEAD_DIM
    conv_dim = xi + 2 * SSD_GROUPS * SSD_STATE
    mod = _ada_mod(c, ada_w, ada_b)
    x2 = x.reshape(bsz * seq, d)
    for i in range(depth):
        shift1, scale1, gate1, shift2, scale2, gate2 = (
            m.reshape(bsz, 1, d) for m in jnp.split(mod[i], 6, axis=-1))
        h = _normmod(x2, norm_mix_g[i], shift1, scale1, seq, transposed=False)
        j = i // 2
        if i % 2 == 0:
            w = in0_w[j]
            dt_cols = slice(xi + conv_dim, xi + conv_dim + SSD_HEADS)
            w_main = jnp.concatenate([w[:, :xi + conv_dim], w[:, dt_cols.stop:]], axis=1).astype(BF16)
            w_dt = jnp.pad(w[:, dt_cols], ((0, 0), (0, LANES - SSD_HEADS))).astype(BF16)
            proj = _matmul([h], [w_main], BF16, 1024, 512, name="in0_proj")
            dt_raw = _matmul([h], [w_dt], F32, 1024, LANES, name="in0_dt")
            ya = _ssd(proj, dt_raw, conv_w[j], conv_b[j], dt_bias[j], a_log[j], d_skip[j], ssd_norm_g[j], seq)
            yb = _gmlp(proj, gmlp_ln_g[j], gmlp_ln_b[j], gmlp_ws[j], gmlp_bs[j])
            wo = out0_w[j].astype(BF16)
            x2 = _matmul([ya, yb], [wo[:xi], wo[xi:]], F32, 1024, 512, resid=x2, gate=gate1, seq=seq,
                         name="out0_proj")
        else:
            qkv = _matmul([h], [sb_qkv_w[j].astype(BF16)], BF16, 1024, 512, name="qkv_proj")
            o = _attn(qkv, seq)
            x2 = _matmul([o], [sb_out_w[j].astype(BF16)], F32, 1024, 512, resid=x2, gate=gate1, seq=seq,
                         name="sb_out_proj")
        x2 = _peer(x2, norm_ffn_g[i], shift2, scale2, gate2, peer_wq[i], peer_keys[i], peer_u[i],
                   peer_v[i], seq)
    return _final_norm(x2, final_g).reshape(bsz, seq, d)
```

```python
import functools

import numpy as np
import jax
import jax.numpy as jnp
from jax import lax
from jax.experimental import pallas as pl
from jax.experimental.pallas import tpu as pltpu

F32 = jnp.float32
BF16 = jnp.bfloat16
EPS = 1e-6
NEG = -1e30
INV_SQRT2 = 0.7071067811865476

SSD_HEADS = 32
SSD_HEAD_DIM = 64
SSD_GROUPS = 8
SSD_STATE = 128
SSD_CONV = 4
SSD_CHUNK = 128
GMLP_GROUPS = 16
GMLP_GROUP_DIM = 128
GMLP_CHUNK = 128
SB_HEADS = 16
SB_HEAD_DIM = 128
PEER_HEADS = 8
PEER_KEYS = 128
PEER_HALF = 128
PEER_TOPK = 16

LANES = 128
SUBLANES = 8
VMEM_LIMIT = 56 * 1024 * 1024


def _params(*sem, vmem=VMEM_LIMIT):
    return pltpu.CompilerParams(dimension_semantics=sem, vmem_limit_bytes=vmem)


def _gelu(x):
    return 0.5 * x * (1.0 + lax.erf(x * INV_SQRT2))


def _silu(x):
    return x * jax.nn.sigmoid(x)


def _softplus(x):
    return jnp.maximum(x, 0.0) + jnp.log1p(jnp.exp(-jnp.abs(x)))


def _split2(v):
    hi = v.astype(BF16)
    lo = (v - hi.astype(F32)).astype(BF16)
    return hi, lo


def _split3(v):
    hi = v.astype(BF16)
    r = v - hi.astype(F32)
    mid = r.astype(BF16)
    lo = (r - mid.astype(F32)).astype(BF16)
    return hi, mid, lo


def _ada_kernel(c_ref, w_ref, b_ref, o_ref):
    cond = _silu(c_ref[...]).astype(BF16)
    o_ref[0] = jnp.dot(cond, w_ref[0].astype(BF16), preferred_element_type=F32) + b_ref[0]


def _ada_mod(c, ada_w, ada_b, tn=512):
    depth, d, n = ada_w.shape
    bsz = c.shape[0]
    rows = -(-bsz // SUBLANES) * SUBLANES
    c_pad = jnp.pad(c, ((0, rows - bsz), (0, 0)))
    out = pl.pallas_call(
        _ada_kernel,
        grid=(depth, n // tn),
        in_specs=[pl.BlockSpec((rows, d), lambda i, j: (0, 0)),
                  pl.BlockSpec((1, d, tn), lambda i, j: (i, 0, j)),
                  pl.BlockSpec((1, 1, tn), lambda i, j: (i, 0, j))],
        out_specs=pl.BlockSpec((1, rows, tn), lambda i, j: (i, 0, j)),
        out_shape=jax.ShapeDtypeStruct((depth, rows, n), F32),
        compiler_params=_params("arbitrary", "arbitrary"),
        name="ada_mod",
    )(c_pad, ada_w, ada_b.reshape(depth, 1, n))
    return out[:, :bsz]


def _normmod_kernel(x_ref, g_ref, sh_ref, sc_ref, o_ref, *, transposed):
    x = x_ref[...]
    y = x * lax.rsqrt(jnp.mean(x * x, axis=-1, keepdims=True) + EPS) * g_ref[...]
    h = y * (1.0 + sc_ref[0]) + sh_ref[0]
    o_ref[...] = (h.T if transposed else h).astype(BF16)


def _normmod(x2, g, shift, scale, seq, transposed, tm=256):
    t, d = x2.shape
    per_batch = seq // tm
    if transposed:
        out_spec = pl.BlockSpec((d, tm), lambda i: (0, i))
        out_shape = jax.ShapeDtypeStruct((d, t), BF16)
    else:
        out_spec = pl.BlockSpec((tm, d), lambda i: (i, 0))
        out_shape = jax.ShapeDtypeStruct((t, d), BF16)
    return pl.pallas_call(
        functools.partial(_normmod_kernel, transposed=transposed),
        grid=(t // tm,),
        in_specs=[pl.BlockSpec((tm, d), lambda i: (i, 0)),
                  pl.BlockSpec((1, d), lambda i: (0, 0)),
                  pl.BlockSpec((1, 1, d), lambda i: (i // per_batch, 0, 0)),
                  pl.BlockSpec((1, 1, d), lambda i: (i // per_batch, 0, 0))],
        out_specs=out_spec,
        out_shape=out_shape,
        compiler_params=_params("arbitrary"),
        name="normmod_t" if transposed else "normmod",
    )(x2, g.reshape(1, d), shift, scale)


def _final_norm_kernel(x_ref, g_ref, o_ref):
    x = x_ref[...]
    o_ref[...] = x * lax.rsqrt(jnp.mean(x * x, axis=-1, keepdims=True) + EPS) * g_ref[...]


def _final_norm(x2, g, tm=256):
    t, d = x2.shape
    return pl.pallas_call(
        _final_norm_kernel,
        grid=(t // tm,),
        in_specs=[pl.BlockSpec((tm, d), lambda i: (i, 0)),
                  pl.BlockSpec((1, d), lambda i: (0, 0))],
        out_specs=pl.BlockSpec((tm, d), lambda i: (i, 0)),
        out_shape=jax.ShapeDtypeStruct((t, d), F32),
        compiler_params=_params("arbitrary"),
        name="final_norm",
    )(x2, g.reshape(1, d))


def _mm_kernel(*refs, n_pairs, has_resid):
    acc = None
    for a_ref, b_ref in zip(refs[:n_pairs], refs[n_pairs:2 * n_pairs]):
        part = jnp.dot(a_ref[...], b_ref[...], preferred_element_type=F32)
        acc = part if acc is None else acc + part
    if has_resid:
        x_ref, gate_ref, o_ref = refs[2 * n_pairs:]
        o_ref[...] = x_ref[...] + gate_ref[0] * acc
    else:
        o_ref = refs[2 * n_pairs]
        o_ref[...] = acc.astype(o_ref.dtype)


def _matmul(a_list, b_list, out_dtype, tm, tn, resid=None, gate=None, seq=None, name="matmul"):
    m = a_list[0].shape[0]
    n = b_list[0].shape[1]
    tm, tn = min(tm, m if seq is None else seq), min(tn, n)
    in_specs = [pl.BlockSpec((tm, a.shape[1]), lambda i, j: (i, 0)) for a in a_list]
    in_specs += [pl.BlockSpec((b.shape[0], tn), lambda i, j: (0, j)) for b in b_list]
    args = list(a_list) + list(b_list)
    if resid is not None:
        per_batch = seq // tm
        in_specs += [pl.BlockSpec((tm, tn), lambda i, j: (i, j)),
                     pl.BlockSpec((1, 1, tn), lambda i, j: (i // per_batch, 0, j))]
        args += [resid, gate]
    return pl.pallas_call(
        functools.partial(_mm_kernel, n_pairs=len(a_list), has_resid=resid is not None),
        grid=(m // tm, n // tn),
        in_specs=in_specs,
        out_specs=pl.BlockSpec((tm, tn), lambda i, j: (i, j)),
        out_shape=jax.ShapeDtypeStruct((m, n), out_dtype),
        compiler_params=_params("arbitrary", "arbitrary"),
        name=name,
    )(*args)


def _ssd_kernel(z_ref, xs_ref, bc_ref, dt_ref, cw_ref, cb_ref, dtb_ref, alog_ref, dskip_ref, ng_ref,
                expand_ref, o_ref, xbuf, bcbuf, ybuf, state_ref):
    L, H, P, G, N = SSD_CHUNK, SSD_HEADS, SSD_HEAD_DIM, SSD_GROUPS, SSD_STATE
    R = H // G
    XI = H * P
    HALO = SUBLANES

    @pl.when(pl.program_id(1) == 0)
    def _():
        xbuf[0:HALO, :] = jnp.zeros((HALO, XI), F32)
        bcbuf[0:HALO, :] = jnp.zeros((HALO, 2 * G * N), F32)
        state_ref[...] = jnp.zeros(state_ref.shape, F32)

    xbuf[HALO:HALO + L, :] = xs_ref[...].astype(F32)
    bcbuf[HALO:HALO + L, :] = bc_ref[...].astype(F32)

    def conv_silu(buf, w, b):
        acc = b
        for k in range(SSD_CONV):
            off = HALO - (SSD_CONV - 1) + k
            acc = acc + w[k:k + 1, :] * buf[off:off + L, :]
        return _silu(acc)

    cw = cw_ref[...]
    cb = cb_ref[...]
    xs = conv_silu(xbuf, cw[:, :XI], cb[:, :XI])
    bc = conv_silu(bcbuf, cw[:, XI:], cb[:, XI:])
    xbuf[0:HALO, :] = xbuf[L:L + HALO, :]
    bcbuf[0:HALO, :] = bcbuf[L:L + HALO, :]

    dt = _softplus(dt_ref[...] + dtb_ref[...])
    a = dt * (-jnp.exp(alog_ref[...]))
    row = lax.broadcasted_iota(jnp.int32, (L, L), 0)
    col = lax.broadcasted_iota(jnp.int32, (L, L), 1)
    causal = col <= row
    tri = jnp.where(causal, 1.0, 0.0).astype(BF16)
    a_cum = sum(jnp.dot(tri, part, preferred_element_type=F32) for part in _split3(a))
    a_cum_t = a_cum.T
    a_last = a_cum[L - 1:L, :]
    dte = jnp.exp(a_last - a_cum)
    eac = jnp.exp(a_cum)

    stk = jnp.concatenate([dt, dt * dte, eac], axis=0)
    hi, lo = _split2(stk)
    ex = jnp.dot(jnp.concatenate([hi, lo], axis=1), expand_ref[...], preferred_element_type=F32)
    dt_e, dd_e, eac_e = ex[0:L], ex[L:2 * L], ex[2 * L:3 * L]
    x_in = xs * dt_e
    xd = (xs * dd_e).astype(BF16)
    cdec_e = eac_e[L - 1:L, :]

    GW = R * P
    lane_head = lax.broadcasted_iota(jnp.int32, (L, GW), 1) // P
    for g in range(G):
        lanes = slice(g * GW, (g + 1) * GW)
        bg = bc[:, g * N:(g + 1) * N]
        cg = bc[:, G * N + g * N:G * N + (g + 1) * N].astype(BF16)
        cbm = lax.dot_general(cg, bg.astype(BF16), (((1,), (1,)), ((), ())), preferred_element_type=F32)
        xg = x_in[:, lanes]
        ms, xblocks = [], []
        for r in range(R):
            h = g * R + r
            seg = a_cum[:, h:h + 1] - a_cum_t[h:h + 1, :]
            dec = jnp.where(causal, jnp.exp(jnp.where(causal, seg, 0.0)), 0.0)
            ms.append((cbm * dec).astype(BF16))
            xblocks.append(jnp.where(lane_head == r, xg, 0.0).astype(BF16))
        y_diag = jnp.dot(jnp.concatenate(ms, axis=1), jnp.concatenate(xblocks, axis=0),
                         preferred_element_type=F32)
        hg = state_ref[g]
        y_off = jnp.dot(cg, hg.astype(BF16), preferred_element_type=F32) * eac_e[:, lanes]
        new_states = jnp.dot(bg.T.astype(BF16), xd[:, lanes], preferred_element_type=F32)
        state_ref[g] = hg * cdec_e[:, lanes] + new_states
        ybuf[:, lanes] = y_diag + y_off + dskip_ref[:, lanes] * xs[:, lanes]

    z = z_ref[...].astype(F32)
    yz = ybuf[...] * _silu(z)
    ya = yz * lax.rsqrt(jnp.mean(yz * yz, axis=-1, keepdims=True) + EPS) * ng_ref[...]
    o_ref[...] = ya.astype(BF16)


def _ssd(proj, dt_raw, conv_w, conv_b, dt_bias, a_log, d_skip, norm_g, seq):
    t = proj.shape[0]
    L, H, P, G, N = SSD_CHUNK, SSD_HEADS, SSD_HEAD_DIM, SSD_GROUPS, SSD_STATE
    xi = H * P
    bcw = 2 * G * N
    nc = seq // L
    pad = LANES - H
    expand = np.zeros((LANES, xi), np.float32)
    expand[np.arange(xi) // P, np.arange(xi)] = 1.0
    expand2 = jnp.asarray(np.concatenate([expand, expand], axis=0), BF16)
    row = lambda v: jnp.pad(v.astype(F32), (0, pad)).reshape(1, LANES)
    full = lambda shape: pl.BlockSpec(shape, lambda b, c: (0, 0))
    return pl.pallas_call(
        _ssd_kernel,
        grid=(t // seq, nc),
        in_specs=[pl.BlockSpec((L, xi), lambda b, c: (b * nc + c, 0)),
                  pl.BlockSpec((L, xi), lambda b, c: (b * nc + c, 1)),
                  pl.BlockSpec((L, bcw), lambda b, c: (b * nc + c, 2)),
                  pl.BlockSpec((L, LANES), lambda b, c: (b * nc + c, 0)),
                  full((SSD_CONV, xi + bcw)), full((1, xi + bcw)),
                  full((1, LANES)), full((1, LANES)), full((1, xi)), full((1, xi)),
                  full((2 * LANES, xi))],
        out_specs=pl.BlockSpec((L, xi), lambda b, c: (b * nc + c, 0)),
        out_shape=jax.ShapeDtypeStruct((t, xi), BF16),
        scratch_shapes=[pltpu.VMEM((SUBLANES + L, xi), F32),
                        pltpu.VMEM((SUBLANES + L, bcw), F32),
                        pltpu.VMEM((L, xi), F32),
                        pltpu.VMEM((G, N, (H // G) * P), F32)],
        compiler_params=_params("arbitrary", "arbitrary"),
        name="ssd",
    )(proj, proj, proj, dt_raw, conv_w, conv_b.reshape(1, -1), row(dt_bias), row(a_log),
      jnp.repeat(d_skip.astype(F32), P).reshape(1, xi), norm_g.reshape(1, xi), expand2)


def _gmlp_kernel(u_ref, v_ref, lng_ref, lnb_ref, ws_ref, bs_ref, o_ref, *, nchunk):
    L, NG, GD = GMLP_CHUNK, GMLP_GROUPS, GMLP_GROUP_DIM
    row = lax.broadcasted_iota(jnp.int32, (L, L), 0)
    col = lax.broadcasted_iota(jnp.int32, (L, L), 1)
    tril = col <= row
    for ch in range(nchunk):
        rows = slice(ch * L, (ch + 1) * L)
        ug = _gelu(u_ref[rows, :].astype(F32))
        vg = _gelu(v_ref[rows, :].astype(F32))
        mu = jnp.mean(vg, axis=-1, keepdims=True)
        xc = vg - mu
        var = jnp.mean(xc * xc, axis=-1, keepdims=True)
        vn = (xc * lax.rsqrt(var + EPS) * lng_ref[...] + lnb_ref[...]).astype(BF16)
        for g in range(NG):
            cols = slice(g * GD, (g + 1) * GD)
            w = jnp.where(tril, ws_ref[g], 0.0).astype(BF16)
            vv = jnp.dot(w, vn[:, cols], preferred_element_type=F32) + bs_ref[:, cols]
            o_ref[rows, cols] = (ug[:, cols] * vv).astype(BF16)


def _gmlp(proj, ln_g, ln_b, ws, bs, nchunk=2):
    t = proj.shape[0]
    L, NG, GD = GMLP_CHUNK, GMLP_GROUPS, GMLP_GROUP_DIM
    w = NG * GD
    tm = nchunk * L
    bs_full = jnp.repeat(bs.T.astype(F32), GD, axis=1)
    return pl.pallas_call(
        functools.partial(_gmlp_kernel, nchunk=nchunk),
        grid=(t // tm,),
        in_specs=[pl.BlockSpec((tm, w), lambda i: (i, 3)),
                  pl.BlockSpec((tm, w), lambda i: (i, 4)),
                  pl.BlockSpec((1, w), lambda i: (0, 0)),
                  pl.BlockSpec((1, w), lambda i: (0, 0)),
                  pl.BlockSpec((NG, L, L), lambda i: (0, 0, 0)),
                  pl.BlockSpec((L, w), lambda i: (0, 0))],
        out_specs=pl.BlockSpec((tm, w), lambda i: (i, 0)),
        out_shape=jax.ShapeDtypeStruct((t, w), BF16),
        compiler_params=_params("arbitrary"),
        name="gmlp",
    )(proj, proj, ln_g.reshape(1, w), ln_b.reshape(1, w), ws, bs_full)


def _attn_kernel(q_ref, k_ref, v_ref, o_ref, *, tq, strip):
    i = pl.program_id(2)
    n_strips = tq // strip
    r2 = lax.broadcasted_iota(jnp.int32, (tq, tq), 0)
    c2 = lax.broadcasted_iota(jnp.int32, (tq, tq), 1)
    after = jnp.where(r2 >= c2, 1.0, 0.0).astype(BF16)
    row = lax.broadcasted_iota(jnp.int32, (strip, tq), 0)
    col = lax.broadcasted_iota(jnp.int32, (strip, tq), 1)

    def tiles(js, carry, diag):
        kbs, vbs = [], []
        for j in js:
            start = pl.multiple_of(j * tq, tq)
            kbs.append(k_ref[pl.ds(start, tq), :])
            vbs.append(v_ref[pl.ds(start, tq), :])
        chains = [(s, b) for s in range(n_strips) for b in range(len(js))]
        masks = {(s, b): (col < row + s * strip) if (diag and b == 0) else None for s, b in chains}
        logits = {(s, b): lax.dot_general(q_ref[s * strip:(s + 1) * strip, :], kbs[b], (((1,), (1,)), ((), ())),
                                          preferred_element_type=F32) for s, b in chains}
        log_1m, split = {}, {}
        for c in chains:
            m = -(jnp.maximum(logits[c], 0.0) + jnp.log(1.0 + jnp.exp(-jnp.abs(logits[c]))))
            if masks[c] is not None:
                m = jnp.where(masks[c], m, 0.0)
            log_1m[c] = m
            split[c] = m.astype(BF16)
        tail = {c: jnp.dot(split[c], after, preferred_element_type=F32) for c in chains}
        runs = {}
        new_run = []
        for s in range(n_strips):
            run = carry[2 * s + 1]
            for b in range(len(js)):
                runs[(s, b)] = run
                run = run + jnp.sum(log_1m[(s, b)], axis=-1, keepdims=True)
            new_run.append(run)
        weights = {}
        for c in chains:
            w = jnp.exp(logits[c] + tail[c] + runs[c])
            if masks[c] is not None:
                w = jnp.where(masks[c], w, 0.0)
            weights[c] = w.astype(BF16)
        pv = {c: jnp.dot(weights[c], vbs[c[1]], preferred_element_type=F32) for c in chains}
        out = []
        for s in range(n_strips):
            acc = carry[2 * s]
            for b in range(len(js)):
                acc = acc + pv[(s, b)]
            out.extend([acc, new_run[s]])
        return tuple(out)

    init = tuple(jnp.zeros((strip, SB_HEAD_DIM if k % 2 == 0 else 1), F32) for k in range(2 * n_strips))
    carry = tiles([i], init, True)
    carry = lax.cond(i % 2 == 1, lambda c: tiles([i - 1], c, False), lambda c: c, carry)
    first = i - 1 - i % 2
    carry = lax.fori_loop(0, i // 2, lambda p, c: tiles([first - 2 * p, first - 2 * p - 1], c, False), carry)
    for s in range(n_strips):
        o_ref[s * strip:(s + 1) * strip, :] = carry[2 * s].astype(BF16)


def _attn(qkv, seq, tq=256, strip=128):
    t = qkv.shape[0]
    hd, nh = SB_HEAD_DIM, SB_HEADS
    nq = seq // tq
    return pl.pallas_call(
        functools.partial(_attn_kernel, tq=tq, strip=min(strip, tq)),
        grid=(t // seq, nh, nq),
        in_specs=[pl.BlockSpec((tq, hd), lambda b, h, i: (b * nq + i, h)),
                  pl.BlockSpec((seq, hd), lambda b, h, i: (b, nh + h)),
                  pl.BlockSpec((seq, hd), lambda b, h, i: (b, 2 * nh + h))],
        out_specs=pl.BlockSpec((tq, hd), lambda b, h, i: (b * nq + i, h)),
        out_shape=jax.ShapeDtypeStruct((t, nh * hd), BF16),
        compiler_params=_params("arbitrary", "arbitrary", "arbitrary"),
        name="sb_attn",
    )(qkv, qkv, qkv)


def _extract_top(work, count):
    tops = []
    for k in range(count):
        m = jnp.max(work, axis=0, keepdims=True)
        tops.append(m)
        if k + 1 < count:
            work = jnp.where(work == m, NEG, work)
    return tops


def _peer_topk_kernel(ht_ref, wqt_ref, keys_ref, d0_ref, p0_ref, s1_ref, p1z_ref, q_scr, cand_scr):
    nh, kd, topk = PEER_HEADS, PEER_HALF, PEER_TOPK
    n_cand = sum((topk + 1) // (k + 1) for k in range(topk + 1))
    q_scr[...] = jnp.dot(wqt_ref[...], ht_ref[...], preferred_element_type=F32).astype(BF16)
    cand_scr[...] = jnp.full(cand_scr.shape, NEG, F32)

    def head(h, carry):
        scores, tops = [], []
        for half in range(2):
            start = pl.multiple_of(h * (2 * kd) + half * kd, kd)
            sc = jnp.dot(keys_ref[h, half], q_scr[pl.ds(start, kd), :], preferred_element_type=F32)
            scores.append(sc)
            tops.append(_extract_top(sc, topk + 1))
        r = 0
        for k in range(topk + 1):
            for l in range((topk + 1) // (k + 1)):
                cand_scr[pl.ds(r, 1), :] = tops[0][k] + tops[1][l]
                r += 1
        cand = cand_scr[...]
        best = _extract_top(cand, topk + 1)
        theta = 0.5 * (best[topk - 1] + best[topk])
        zsum = jnp.sum(jnp.where(cand > theta, jnp.exp(cand - best[0]), 0.0), axis=0, keepdims=True)
        d0_ref[h] = theta - scores[0]
        p0_ref[h] = jnp.exp(scores[0] - tops[0][0])
        s1_ref[h] = scores[1]
        p1z_ref[h] = jnp.exp(scores[1] - tops[1][0]) / zsum
        return carry

    assert n_cand <= cand_scr.shape[0]
    lax.fori_loop(0, nh, head, 0)


def _peer_topk(ht, wqt, keys, tb=256):
    d, t = ht.shape
    nh, nk, kd = PEER_HEADS, PEER_KEYS, PEER_HALF
    n_cand = sum((PEER_TOPK + 1) // (k + 1) for k in range(PEER_TOPK + 1))
    cand_rows = -(-n_cand // SUBLANES) * SUBLANES
    out_spec = pl.BlockSpec((nh, nk, tb), lambda i: (0, 0, i))
    out_shape = jax.ShapeDtypeStruct((nh, nk, t), F32)
    return pl.pallas_call(
        _peer_topk_kernel,
        grid=(t // tb,),
        in_specs=[pl.BlockSpec((d, tb), lambda i: (0, i)),
                  pl.BlockSpec((nh * 2 * kd, d), lambda i: (0, 0)),
                  pl.BlockSpec((nh, 2, nk, kd), lambda i: (0, 0, 0, 0))],
        out_specs=[out_spec] * 4,
        out_shape=[out_shape] * 4,
        scratch_shapes=[pltpu.VMEM((nh * 2 * kd, tb), BF16),
                        pltpu.VMEM((cand_rows, tb), F32)],
        compiler_params=_params("arbitrary"),
        name="peer_topk",
    )(ht, wqt, keys)


def _peer_dense_kernel(ht_ref, u_ref, vt_ref, d0_ref, p0_ref, s1_ref, p1z_ref, x_ref, gate_ref, o_ref,
                       acc_ref, g0_scr, g1_scr, *, ni, tb, n_blocks):
    nh, nk = PEER_HEADS, PEER_KEYS
    d = acc_ref.shape[0]
    e = pl.program_id(1)
    pair = 2 * nk
    n_chunks = ni * nk // pair
    acc_rows = d // n_chunks

    @pl.when(e == 0)
    def _():
        acc_ref[...] = jnp.zeros(acc_ref.shape, F32)
        g1_scr[...] = jnp.zeros(g1_scr.shape, BF16)

    def gate_tile(ii, cs):
        w = None
        for h in range(nh):
            thr = d0_ref[ii, h:h + 1, cs]
            p0 = p0_ref[ii, h:h + 1, cs]
            term = jnp.where(s1_ref[h, :, cs] >= thr, p1z_ref[h, :, cs], 0.0) * p0
            w = term if w is None else w + term
        return w

    def step(g_cur, g_prev):
        for c in range(n_chunks):
            a = jnp.dot(u_ref[c * pair:(c + 1) * pair, :], ht_ref[...], preferred_element_type=F32)
            for sub in range(pair // nk):
                ii = c * (pair // nk) + sub
                for cc in range(tb // LANES):
                    cs = slice(cc * LANES, (cc + 1) * LANES)
                    g = _gelu(a[sub * nk:(sub + 1) * nk, cs]) * gate_tile(ii, cs)
                    g_cur[ii * nk:(ii + 1) * nk, cs] = g.astype(BF16)
            rows = slice(c * acc_rows, (c + 1) * acc_rows)
            acc_ref[rows, :] += jnp.dot(vt_ref[rows, :], g_prev[...], preferred_element_type=F32)

    @pl.when(jnp.logical_and(e < n_blocks, e % 2 == 0))
    def _():
        step(g0_scr, g1_scr)

    @pl.when(jnp.logical_and(e < n_blocks, e % 2 == 1))
    def _():
        step(g1_scr, g0_scr)

    @pl.when(e == n_blocks)
    def _():
        g_last = g1_scr if n_blocks % 2 == 0 else g0_scr
        total = acc_ref[...] + jnp.dot(vt_ref[...], g_last[...], preferred_element_type=F32)
        o_ref[...] = x_ref[...] + gate_ref[0] * total.T


def _peer_dense(ht, u, vt, d0, p0, s1, p1z, x2, gate, seq, tb=512, eb=512):
    d, t = ht.shape
    n_exp = u.shape[0]
    nh, nk = PEER_HEADS, PEER_KEYS
    tb, eb = min(tb, seq), min(eb, n_exp)
    ni = eb // nk
    n_blocks = n_exp // eb
    per_batch = seq // tb
    blk = lambda b: jnp.clip(b, 0, n_blocks - 1)
    score_spec = pl.BlockSpec((nh, nk, tb), lambda i, e: (0, 0, i))
    first_spec = pl.BlockSpec((ni, nh, tb), lambda i, e: (blk(e), 0, i))
    return pl.pallas_call(
        functools.partial(_peer_dense_kernel, ni=ni, tb=tb, n_blocks=n_blocks),
        grid=(t // tb, n_blocks + 1),
        in_specs=[pl.BlockSpec((d, tb), lambda i, e: (0, i)),
                  pl.BlockSpec((eb, d), lambda i, e: (blk(e), 0)),
                  pl.BlockSpec((d, eb), lambda i, e: (0, blk(e - 1))),
                  first_spec, first_spec, score_spec, score_spec,
                  pl.BlockSpec((tb, d), lambda i, e: (i, 0), pipeline_mode=pl.Buffered(1)),
                  pl.BlockSpec((1, 1, d), lambda i, e: (i // per_batch, 0, 0))],
        out_specs=pl.BlockSpec((tb, d), lambda i, e: (i, 0)),
        out_shape=jax.ShapeDtypeStruct((t, d), F32),
        scratch_shapes=[pltpu.VMEM((d, tb), F32),
                        pltpu.VMEM((eb, tb), BF16),
                        pltpu.VMEM((eb, tb), BF16)],
        compiler_params=_params("arbitrary", "arbitrary"),
        name="peer_dense",
    )(ht, u, vt, d0.transpose(1, 0, 2), p0.transpose(1, 0, 2), s1, p1z, x2, gate)


def _peer(x2, g, shift, scale, gate, wq, keys, u, v, seq):
    ht = _normmod(x2, g, shift, scale, seq, transposed=True)
    d0, p0, s1, p1z = _peer_topk(ht, wq.T.astype(BF16), keys.astype(BF16))
    return _peer_dense(ht, u.astype(BF16), v.T.astype(BF16), d0, p0, s1, p1z, x2, gate, seq)


def kernel(x, c, ada_w, ada_b, norm_mix_g, norm_ffn_g, in0_w, conv_w, conv_b, dt_bias, a_log, d_skip,
           ssd_norm_g, gmlp_ln_g, gmlp_ln_b, gmlp_ws, gmlp_bs, out0_w, sb_qkv_w, sb_out_w, peer_wq,
           peer_keys, peer_u, peer_v, final_g):
    bsz, seq, d = x.shape
    depth = ada_w.shape[0]
    xi = SSD_HEADS * SSD_HEAD_DIM
    conv_dim = xi + 2 * SSD_GROUPS * SSD_STATE
    mod = _ada_mod(c, ada_w, ada_b)
    x2 = x.reshape(bsz * seq, d)
    for i in range(depth):
        shift1, scale1, gate1, shift2, scale2, gate2 = (
            m.reshape(bsz, 1, d) for m in jnp.split(mod[i], 6, axis=-1))
        h = _normmod(x2, norm_mix_g[i], shift1, scale1, seq, transposed=False)
        j = i // 2
        if i % 2 == 0:
            w = in0_w[j]
            dt_cols = slice(xi + conv_dim, xi + conv_dim + SSD_HEADS)
            w_main = jnp.concatenate([w[:, :xi + conv_dim], w[:, dt_cols.stop:]], axis=1).astype(BF16)
            w_dt = jnp.pad(w[:, dt_cols], ((0, 0), (0, LANES - SSD_HEADS))).astype(BF16)
            proj = _matmul([h], [w_main], BF16, 1024, 512, name="in0_proj")
            dt_raw = _matmul([h], [w_dt], F32, 1024, LANES, name="in0_dt")
            ya = _ssd(proj, dt_raw, conv_w[j], conv_b[j], dt_bias[j], a_log[j], d_skip[j], ssd_norm_g[j], seq)
            yb = _gmlp(proj, gmlp_ln_g[j], gmlp_ln_b[j], gmlp_ws[j], gmlp_bs[j])
            wo = out0_w[j].astype(BF16)
            x2 = _matmul([ya, yb], [wo[:xi], wo[xi:]], F32, 1024, 512, resid=x2, gate=gate1, seq=seq,
                         name="out0_proj")
        else:
            w_qkv = sb_qkv_w[j]
            n_q = SB_HEADS * SB_HEAD_DIM
            w_qkv = jnp.concatenate([w_qkv[:, :n_q] * SB_HEAD_DIM ** -0.5, w_qkv[:, n_q:]], axis=1).astype(BF16)
            qkv = _matmul([h], [w_qkv], BF16, 1024, 512, name="qkv_proj")
            o = _attn(qkv, seq)
            x2 = _matmul([o], [sb_out_w[j].astype(BF16)], F32, 1024, 512, resid=x2, gate=gate1, seq=seq,
                         name="sb_out_proj")
        x2 = _peer(x2, norm_ffn_g[i], shift2, scale2, gate2, peer_wq[i], peer_keys[i], peer_u[i],
                   peer_v[i], seq)
    return _final_norm(x2, final_g).reshape(bsz, seq, d)
```

```python
import functools

import numpy as np
import jax
import jax.numpy as jnp
from jax import lax
from jax.experimental import pallas as pl
from jax.experimental.pallas import tpu as pltpu

F32 = jnp.float32
BF16 = jnp.bfloat16
EPS = 1e-6
NEG = -1e30
INV_SQRT2 = 0.7071067811865476

SSD_HEADS = 32
SSD_HEAD_DIM = 64
SSD_GROUPS = 8
SSD_STATE = 128
SSD_CONV = 4
SSD_CHUNK = 128
GMLP_GROUPS = 16
GMLP_GROUP_DIM = 128
GMLP_CHUNK = 128
SB_HEADS = 16
SB_HEAD_DIM = 128
PEER_HEADS = 8
PEER_KEYS = 128
PEER_HALF = 128
PEER_TOPK = 16
GATE_ROWS = 32

LANES = 128
SUBLANES = 8
VMEM_LIMIT = 56 * 1024 * 1024


def _params(*sem, vmem=VMEM_LIMIT):
    return pltpu.CompilerParams(dimension_semantics=sem, vmem_limit_bytes=vmem)


def _gelu(x):
    return 0.5 * x * (1.0 + lax.erf(x * INV_SQRT2))


def _silu(x):
    return x * jax.nn.sigmoid(x)


def _softplus(x):
    return jnp.maximum(x, 0.0) + jnp.log1p(jnp.exp(-jnp.abs(x)))


def _split2(v):
    hi = v.astype(BF16)
    lo = (v - hi.astype(F32)).astype(BF16)
    return hi, lo


def _split3(v):
    hi = v.astype(BF16)
    r = v - hi.astype(F32)
    mid = r.astype(BF16)
    lo = (r - mid.astype(F32)).astype(BF16)
    return hi, mid, lo


def _ada_kernel(c_ref, w_ref, b_ref, o_ref):
    cond = _silu(c_ref[...]).astype(BF16)
    o_ref[0] = jnp.dot(cond, w_ref[0].astype(BF16), preferred_element_type=F32) + b_ref[0]


def _ada_mod(c, ada_w, ada_b, tn=512):
    depth, d, n = ada_w.shape
    bsz = c.shape[0]
    rows = -(-bsz // SUBLANES) * SUBLANES
    c_pad = jnp.pad(c, ((0, rows - bsz), (0, 0)))
    out = pl.pallas_call(
        _ada_kernel,
        grid=(depth, n // tn),
        in_specs=[pl.BlockSpec((rows, d), lambda i, j: (0, 0)),
                  pl.BlockSpec((1, d, tn), lambda i, j: (i, 0, j)),
                  pl.BlockSpec((1, 1, tn), lambda i, j: (i, 0, j))],
        out_specs=pl.BlockSpec((1, rows, tn), lambda i, j: (i, 0, j)),
        out_shape=jax.ShapeDtypeStruct((depth, rows, n), F32),
        compiler_params=_params("arbitrary", "arbitrary"),
        name="ada_mod",
    )(c_pad, ada_w, ada_b.reshape(depth, 1, n))
    return out[:, :bsz]


def _normmod_kernel(x_ref, g_ref, sh_ref, sc_ref, o_ref, *, transposed):
    x = x_ref[...]
    y = x * lax.rsqrt(jnp.mean(x * x, axis=-1, keepdims=True) + EPS) * g_ref[...]
    h = y * (1.0 + sc_ref[0]) + sh_ref[0]
    o_ref[...] = (h.T if transposed else h).astype(BF16)


def _normmod(x2, g, shift, scale, seq, transposed, tm=512):
    t, d = x2.shape
    tm = min(tm, seq)
    per_batch = seq // tm
    if transposed:
        out_spec = pl.BlockSpec((d, tm), lambda i: (0, i))
        out_shape = jax.ShapeDtypeStruct((d, t), BF16)
    else:
        out_spec = pl.BlockSpec((tm, d), lambda i: (i, 0))
        out_shape = jax.ShapeDtypeStruct((t, d), BF16)
    return pl.pallas_call(
        functools.partial(_normmod_kernel, transposed=transposed),
        grid=(t // tm,),
        in_specs=[pl.BlockSpec((tm, d), lambda i: (i, 0)),
                  pl.BlockSpec((1, d), lambda i: (0, 0)),
                  pl.BlockSpec((1, 1, d), lambda i: (i // per_batch, 0, 0)),
                  pl.BlockSpec((1, 1, d), lambda i: (i // per_batch, 0, 0))],
        out_specs=out_spec,
        out_shape=out_shape,
        compiler_params=_params("arbitrary"),
        name="normmod_t" if transposed else "normmod",
    )(x2, g.reshape(1, d), shift, scale)


def _final_norm_kernel(x_ref, g_ref, o_ref):
    x = x_ref[...]
    o_ref[...] = x * lax.rsqrt(jnp.mean(x * x, axis=-1, keepdims=True) + EPS) * g_ref[...]


def _final_norm(x2, g, tm=512):
    t, d = x2.shape
    tm = min(tm, t)
    return pl.pallas_call(
        _final_norm_kernel,
        grid=(t // tm,),
        in_specs=[pl.BlockSpec((tm, d), lambda i: (i, 0)),
                  pl.BlockSpec((1, d), lambda i: (0, 0))],
        out_specs=pl.BlockSpec((tm, d), lambda i: (i, 0)),
        out_shape=jax.ShapeDtypeStruct((t, d), F32),
        compiler_params=_params("arbitrary"),
        name="final_norm",
    )(x2, g.reshape(1, d))


def _mm_kernel(*refs, n_pairs, has_resid):
    acc = None
    for a_ref, b_ref in zip(refs[:n_pairs], refs[n_pairs:2 * n_pairs]):
        part = jnp.dot(a_ref[...], b_ref[...], preferred_element_type=F32)
        acc = part if acc is None else acc + part
    if has_resid:
        x_ref, gate_ref, o_ref = refs[2 * n_pairs:]
        o_ref[...] = x_ref[...] + gate_ref[0] * acc
    else:
        o_ref = refs[2 * n_pairs]
        o_ref[...] = acc.astype(o_ref.dtype)


def _matmul(a_list, b_list, out_dtype, tm, tn, resid=None, gate=None, seq=None, name="matmul"):
    m = a_list[0].shape[0]
    n = b_list[0].shape[1]
    tm, tn = min(tm, m if seq is None else seq), min(tn, n)
    in_specs = [pl.BlockSpec((tm, a.shape[1]), lambda i, j: (i, 0)) for a in a_list]
    in_specs += [pl.BlockSpec((b.shape[0], tn), lambda i, j: (0, j)) for b in b_list]
    args = list(a_list) + list(b_list)
    if resid is not None:
        per_batch = seq // tm
        in_specs += [pl.BlockSpec((tm, tn), lambda i, j: (i, j)),
                     pl.BlockSpec((1, 1, tn), lambda i, j: (i // per_batch, 0, j))]
        args += [resid, gate]
    return pl.pallas_call(
        functools.partial(_mm_kernel, n_pairs=len(a_list), has_resid=resid is not None),
        grid=(m // tm, n // tn),
        in_specs=in_specs,
        out_specs=pl.BlockSpec((tm, tn), lambda i, j: (i, j)),
        out_shape=jax.ShapeDtypeStruct((m, n), out_dtype),
        compiler_params=_params("arbitrary", "arbitrary"),
        name=name,
    )(*args)


def _ssd_kernel(z_ref, xs_ref, bc_ref, dt_ref, cw_ref, cb_ref, dtb_ref, alog_ref, dskip_ref, ng_ref,
                expand_ref, o_ref, xbuf, bcbuf, ybuf, state_ref):
    L, H, P, G, N = SSD_CHUNK, SSD_HEADS, SSD_HEAD_DIM, SSD_GROUPS, SSD_STATE
    R = H // G
    XI = H * P
    HALO = SUBLANES

    @pl.when(pl.program_id(1) == 0)
    def _():
        xbuf[0:HALO, :] = jnp.zeros((HALO, XI), F32)
        bcbuf[0:HALO, :] = jnp.zeros((HALO, 2 * G * N), F32)
        state_ref[...] = jnp.zeros(state_ref.shape, F32)

    xbuf[HALO:HALO + L, :] = xs_ref[...].astype(F32)
    bcbuf[HALO:HALO + L, :] = bc_ref[...].astype(F32)

    def conv_silu(buf, w, b):
        acc = b
        for k in range(SSD_CONV):
            off = HALO - (SSD_CONV - 1) + k
            acc = acc + w[k:k + 1, :] * buf[off:off + L, :]
        return _silu(acc)

    cw = cw_ref[...]
    cb = cb_ref[...]
    xs = conv_silu(xbuf, cw[:, :XI], cb[:, :XI])
    bc = conv_silu(bcbuf, cw[:, XI:], cb[:, XI:])
    xbuf[0:HALO, :] = xbuf[L:L + HALO, :]
    bcbuf[0:HALO, :] = bcbuf[L:L + HALO, :]

    dt = _softplus(dt_ref[...] + dtb_ref[...])
    a = dt * (-jnp.exp(alog_ref[...]))
    row = lax.broadcasted_iota(jnp.int32, (L, L), 0)
    col = lax.broadcasted_iota(jnp.int32, (L, L), 1)
    causal = col <= row
    tri = jnp.where(causal, 1.0, 0.0).astype(BF16)
    a_cum = sum(jnp.dot(tri, part, preferred_element_type=F32) for part in _split3(a))
    a_cum_t = a_cum.T
    a_last = a_cum[L - 1:L, :]
    dte = jnp.exp(a_last - a_cum)
    eac = jnp.exp(a_cum)

    stk = jnp.concatenate([dt, dt * dte, eac], axis=0)
    hi, lo = _split2(stk)
    ex = jnp.dot(jnp.concatenate([hi, lo], axis=1), expand_ref[...], preferred_element_type=F32)
    dt_e, dd_e, eac_e = ex[0:L], ex[L:2 * L], ex[2 * L:3 * L]
    x_in = xs * dt_e
    xd = (xs * dd_e).astype(BF16)
    cdec_e = eac_e[L - 1:L, :]

    GW = R * P
    lane_head = lax.broadcasted_iota(jnp.int32, (L, GW), 1) // P
    for g in range(G):
        lanes = slice(g * GW, (g + 1) * GW)
        bg = bc[:, g * N:(g + 1) * N]
        cg = bc[:, G * N + g * N:G * N + (g + 1) * N].astype(BF16)
        cbm = lax.dot_general(cg, bg.astype(BF16), (((1,), (1,)), ((), ())), preferred_element_type=F32)
        xg = x_in[:, lanes]
        ms, xblocks = [], []
        for r in range(R):
            h = g * R + r
            seg = a_cum[:, h:h + 1] - a_cum_t[h:h + 1, :]
            dec = jnp.where(causal, jnp.exp(jnp.where(causal, seg, 0.0)), 0.0)
            ms.append((cbm * dec).astype(BF16))
            xblocks.append(jnp.where(lane_head == r, xg, 0.0).astype(BF16))
        y_diag = jnp.dot(jnp.concatenate(ms, axis=1), jnp.concatenate(xblocks, axis=0),
                         preferred_element_type=F32)
        hg = state_ref[g]
        y_off = jnp.dot(cg, hg.astype(BF16), preferred_element_type=F32) * eac_e[:, lanes]
        new_states = jnp.dot(bg.T.astype(BF16), xd[:, lanes], preferred_element_type=F32)
        state_ref[g] = hg * cdec_e[:, lanes] + new_states
        ybuf[:, lanes] = y_diag + y_off + dskip_ref[:, lanes] * xs[:, lanes]

    z = z_ref[...].astype(F32)
    yz = ybuf[...] * _silu(z)
    ya = yz * lax.rsqrt(jnp.mean(yz * yz, axis=-1, keepdims=True) + EPS) * ng_ref[...]
    o_ref[...] = ya.astype(BF16)


def _ssd(proj, dt_raw, conv_w, conv_b, dt_bias, a_log, d_skip, norm_g, seq):
    t = proj.shape[0]
    L, H, P, G, N = SSD_CHUNK, SSD_HEADS, SSD_HEAD_DIM, SSD_GROUPS, SSD_STATE
    xi = H * P
    bcw = 2 * G * N
    nc = seq // L
    pad = LANES - H
    expand = np.zeros((LANES, xi), np.float32)
    expand[np.arange(xi) // P, np.arange(xi)] = 1.0
    expand2 = jnp.asarray(np.concatenate([expand, expand], axis=0), BF16)
    row = lambda v: jnp.pad(v.astype(F32), (0, pad)).reshape(1, LANES)
    full = lambda shape: pl.BlockSpec(shape, lambda b, c: (0, 0))
    return pl.pallas_call(
        _ssd_kernel,
        grid=(t // seq, nc),
        in_specs=[pl.BlockSpec((L, xi), lambda b, c: (b * nc + c, 0)),
                  pl.BlockSpec((L, xi), lambda b, c: (b * nc + c, 1)),
                  pl.BlockSpec((L, bcw), lambda b, c: (b * nc + c, 2)),
                  pl.BlockSpec((L, LANES), lambda b, c: (b * nc + c, 0)),
                  full((SSD_CONV, xi + bcw)), full((1, xi + bcw)),
                  full((1, LANES)), full((1, LANES)), full((1, xi)), full((1, xi)),
                  full((2 * LANES, xi))],
        out_specs=pl.BlockSpec((L, xi), lambda b, c: (b * nc + c, 0)),
        out_shape=jax.ShapeDtypeStruct((t, xi), BF16),
        scratch_shapes=[pltpu.VMEM((SUBLANES + L, xi), F32),
                        pltpu.VMEM((SUBLANES + L, bcw), F32),
                        pltpu.VMEM((L, xi), F32),
                        pltpu.VMEM((G, N, (H // G) * P), F32)],
        compiler_params=_params("arbitrary", "arbitrary"),
        name="ssd",
    )(proj, proj, proj, dt_raw, conv_w, conv_b.reshape(1, -1), row(dt_bias), row(a_log),
      jnp.repeat(d_skip.astype(F32), P).reshape(1, xi), norm_g.reshape(1, xi), expand2)


def _gmlp_kernel(u_ref, v_ref, lng_ref, lnb_ref, ws_ref, bs_ref, o_ref, *, nchunk):
    L, NG, GD = GMLP_CHUNK, GMLP_GROUPS, GMLP_GROUP_DIM
    row = lax.broadcasted_iota(jnp.int32, (L, L), 0)
    col = lax.broadcasted_iota(jnp.int32, (L, L), 1)
    tril = col <= row
    for ch in range(nchunk):
        rows = slice(ch * L, (ch + 1) * L)
        ug = _gelu(u_ref[rows, :].astype(F32))
        vg = _gelu(v_ref[rows, :].astype(F32))
        mu = jnp.mean(vg, axis=-1, keepdims=True)
        xc = vg - mu
        var = jnp.mean(xc * xc, axis=-1, keepdims=True)
        vn = (xc * lax.rsqrt(var + EPS) * lng_ref[...] + lnb_ref[...]).astype(BF16)
        for g in range(NG):
            cols = slice(g * GD, (g + 1) * GD)
            w = jnp.where(tril, ws_ref[g], 0.0).astype(BF16)
            vv = jnp.dot(w, vn[:, cols], preferred_element_type=F32) + bs_ref[:, cols]
            o_ref[rows, cols] = (ug[:, cols] * vv).astype(BF16)


def _gmlp(proj, ln_g, ln_b, ws, bs, nchunk=2):
    t = proj.shape[0]
    L, NG, GD = GMLP_CHUNK, GMLP_GROUPS, GMLP_GROUP_DIM
    w = NG * GD
    tm = nchunk * L
    bs_full = jnp.repeat(bs.T.astype(F32), GD, axis=1)
    return pl.pallas_call(
        functools.partial(_gmlp_kernel, nchunk=nchunk),
        grid=(t // tm,),
        in_specs=[pl.BlockSpec((tm, w), lambda i: (i, 3)),
                  pl.BlockSpec((tm, w), lambda i: (i, 4)),
                  pl.BlockSpec((1, w), lambda i: (0, 0)),
                  pl.BlockSpec((1, w), lambda i: (0, 0)),
                  pl.BlockSpec((NG, L, L), lambda i: (0, 0, 0)),
                  pl.BlockSpec((L, w), lambda i: (0, 0))],
        out_specs=pl.BlockSpec((tm, w), lambda i: (i, 0)),
        out_shape=jax.ShapeDtypeStruct((t, w), BF16),
        compiler_params=_params("arbitrary"),
        name="gmlp",
    )(proj, proj, ln_g.reshape(1, w), ln_b.reshape(1, w), ws, bs_full)


def _attn_kernel(q_ref, k_ref, v_ref, o_ref, *, tq, strip):
    i = pl.program_id(2)
    tk = tq // 2
    n_strips = tq // strip
    r2 = lax.broadcasted_iota(jnp.int32, (tk, tk), 0)
    c2 = lax.broadcasted_iota(jnp.int32, (tk, tk), 1)
    after = jnp.where(r2 >= c2, 1.0, 0.0).astype(BF16)
    row = lax.broadcasted_iota(jnp.int32, (strip, tk), 0)
    col = lax.broadcasted_iota(jnp.int32, (strip, tk), 1)

    def tiles(js, carry, diag):
        kbs = [k_ref[pl.ds(pl.multiple_of(j * tk, tk), tk), :] for j in js]
        vbs = [v_ref[pl.ds(pl.multiple_of(j * tk, tk), tk), :] for j in js]
        chains, masks = [], {}
        for s in range(n_strips):
            for b in range(len(js)):
                shift = s * strip - (len(js) - 1 - b) * tk if diag else tk
                if shift + strip - 1 <= 0:
                    continue
                chains.append((s, b))
                masks[(s, b)] = None if shift >= tk else (col < row + shift)
        logits = {(s, b): lax.dot_general(q_ref[s * strip:(s + 1) * strip, :], kbs[b], (((1,), (1,)), ((), ())),
                                          preferred_element_type=F32) for s, b in chains}
        log_1m, split = {}, {}
        for c in chains:
            m = -(jnp.maximum(logits[c], 0.0) + jnp.log(1.0 + jnp.exp(-jnp.abs(logits[c]))))
            if masks[c] is not None:
                m = jnp.where(masks[c], m, 0.0)
            log_1m[c] = m
            split[c] = m.astype(BF16)
        tail = {c: jnp.dot(split[c], after, preferred_element_type=F32) for c in chains}
        runs = {}
        new_run = []
        for s in range(n_strips):
            run = carry[2 * s + 1]
            for b in range(len(js)):
                if (s, b) in masks:
                    runs[(s, b)] = run
                    run = run + jnp.sum(log_1m[(s, b)], axis=-1, keepdims=True)
            new_run.append(run)
        weights = {}
        for c in chains:
            w = jnp.exp(logits[c] + tail[c] + runs[c])
            if masks[c] is not None:
                w = jnp.where(masks[c], w, 0.0)
            weights[c] = w.astype(BF16)
        pv = {c: jnp.dot(weights[c], vbs[c[1]], preferred_element_type=F32) for c in chains}
        out = []
        for s in range(n_strips):
            acc = carry[2 * s]
            for b in range(len(js)):
                if (s, b) in pv:
                    acc = acc + pv[(s, b)]
            out.extend([acc, new_run[s]])
        return tuple(out)

    init = tuple(jnp.zeros((strip, SB_HEAD_DIM if k % 2 == 0 else 1), F32) for k in range(2 * n_strips))
    carry = tiles([2 * i + 1, 2 * i], init, True)
    carry = lax.fori_loop(0, i, lambda p, c: tiles([2 * (i - p) - 1, 2 * (i - p) - 2], c, False), carry)
    for s in range(n_strips):
        o_ref[s * strip:(s + 1) * strip, :] = carry[2 * s].astype(BF16)


def _attn(qkv, seq, tq=512, strip=128):
    t = qkv.shape[0]
    hd, nh = SB_HEAD_DIM, SB_HEADS
    tq = min(tq, seq)
    nq = seq // tq
    return pl.pallas_call(
        functools.partial(_attn_kernel, tq=tq, strip=min(strip, tq)),
        grid=(t // seq, nh, nq),
        in_specs=[pl.BlockSpec((tq, hd), lambda b, h, i: (b * nq + i, h)),
                  pl.BlockSpec((seq, hd), lambda b, h, i: (b, nh + h)),
                  pl.BlockSpec((seq, hd), lambda b, h, i: (b, 2 * nh + h))],
        out_specs=pl.BlockSpec((tq, hd), lambda b, h, i: (b * nq + i, h)),
        out_shape=jax.ShapeDtypeStruct((t, nh * hd), BF16),
        compiler_params=_params("arbitrary", "arbitrary", "arbitrary"),
        name="sb_attn",
    )(qkv, qkv, qkv)


def _extract_top(work, count):
    tops = []
    for k in range(count):
        m = jnp.max(work, axis=0, keepdims=True)
        tops.append(m)
        if k + 1 < count:
            work = jnp.where(work == m, NEG, work)
    return tops


def _peer_topk_kernel(ht_ref, wqt_ref, keys_ref, d0_ref, p0_ref, s1_ref, p1z_ref, q_scr, cand_scr):
    nh, kd, topk = PEER_HEADS, PEER_HALF, PEER_TOPK
    n_cand = sum((topk + 1) // (k + 1) for k in range(topk + 1))
    q_scr[...] = jnp.dot(wqt_ref[...], ht_ref[...], preferred_element_type=F32).astype(BF16)
    cand_scr[...] = jnp.full(cand_scr.shape, NEG, F32)

    def head(h, carry):
        scores, tops = [], []
        for half in range(2):
            start = pl.multiple_of(h * (2 * kd) + half * kd, kd)
            sc = jnp.dot(keys_ref[h, half], q_scr[pl.ds(start, kd), :], preferred_element_type=F32)
            scores.append(sc)
            tops.append(_extract_top(sc, topk + 1))
        r = 0
        for k in range(topk + 1):
            for l in range((topk + 1) // (k + 1)):
                cand_scr[pl.ds(r, 1), :] = tops[0][k] + tops[1][l]
                r += 1
        cand = cand_scr[...]
        best = _extract_top(cand, topk + 1)
        theta = 0.5 * (best[topk - 1] + best[topk])
        zsum = jnp.sum(jnp.where(cand > theta, jnp.exp(cand - best[0]), 0.0), axis=0, keepdims=True)
        d0_ref[h] = theta - scores[0]
        p0_ref[h] = jnp.exp(scores[0] - tops[0][0])
        s1_ref[h] = scores[1]
        p1z_ref[h] = jnp.exp(scores[1] - tops[1][0]) / zsum
        return carry

    assert n_cand <= cand_scr.shape[0]
    lax.fori_loop(0, nh, head, 0)


def _peer_topk(ht, wqt, keys, tb=256):
    d, t = ht.shape
    nh, nk, kd = PEER_HEADS, PEER_KEYS, PEER_HALF
    n_cand = sum((PEER_TOPK + 1) // (k + 1) for k in range(PEER_TOPK + 1))
    cand_rows = -(-n_cand // SUBLANES) * SUBLANES
    out_spec = pl.BlockSpec((nh, nk, tb), lambda i: (0, 0, i))
    out_shape = jax.ShapeDtypeStruct((nh, nk, t), F32)
    return pl.pallas_call(
        _peer_topk_kernel,
        grid=(t // tb,),
        in_specs=[pl.BlockSpec((d, tb), lambda i: (0, i)),
                  pl.BlockSpec((nh * 2 * kd, d), lambda i: (0, 0)),
                  pl.BlockSpec((nh, 2, nk, kd), lambda i: (0, 0, 0, 0))],
        out_specs=[out_spec] * 4,
        out_shape=[out_shape] * 4,
        scratch_shapes=[pltpu.VMEM((nh * 2 * kd, tb), BF16),
                        pltpu.VMEM((cand_rows, tb), F32)],
        compiler_params=_params("arbitrary"),
        name="peer_topk",
    )(ht, wqt, keys)


def _peer_dense_kernel(ht_ref, u_ref, vt_ref, d0_ref, p0_ref, s1_ref, p1z_ref, x_ref, gate_ref, o_ref,
                       acc_ref, g0_scr, g1_scr, a_scr, *, ni, tb, n_blocks):
    nh, nk = PEER_HEADS, PEER_KEYS
    d = acc_ref.shape[0]
    e = pl.program_id(1)
    pair = 2 * nk
    n_chunks = ni * nk // pair
    acc_rows = d // n_chunks

    @pl.when(e == 0)
    def _():
        acc_ref[...] = jnp.zeros(acc_ref.shape, F32)
        g1_scr[...] = jnp.zeros(g1_scr.shape, BF16)

    def gate_chunk(c, g_cur):
        for sub in range(pair // nk):
            ii = c * (pair // nk) + sub
            for cc in range(tb // LANES):
                cs = slice(cc * LANES, (cc + 1) * LANES)
                for j0 in range(0, nk, GATE_ROWS):
                    js = slice(j0, j0 + GATE_ROWS)
                    w = None
                    for h in range(nh):
                        thr = d0_ref[ii, h:h + 1, cs]
                        p0 = p0_ref[ii, h:h + 1, cs]
                        term = jnp.where(s1_ref[h, js, cs] >= thr, p1z_ref[h, js, cs], 0.0) * p0
                        w = term if w is None else w + term
                    a = a_scr[sub * nk + j0:sub * nk + j0 + GATE_ROWS, cs]
                    g_cur[ii * nk + j0:ii * nk + j0 + GATE_ROWS, cs] = (_gelu(a) * w).astype(BF16)

    def first_product(c):
        a_scr[...] = jnp.dot(u_ref[c * pair:(c + 1) * pair, :], ht_ref[...], preferred_element_type=F32)

    def step(g_cur, g_prev):
        for c in range(n_chunks):
            first_product(c)
            rows = slice(c * acc_rows, (c + 1) * acc_rows)
            acc_ref[rows, :] += jnp.dot(vt_ref[rows, :], g_prev[...], preferred_element_type=F32)
            gate_chunk(c, g_cur)

    @pl.when(jnp.logical_and(e < n_blocks, e % 2 == 0))
    def _():
        step(g0_scr, g1_scr)

    @pl.when(jnp.logical_and(e < n_blocks, e % 2 == 1))
    def _():
        step(g1_scr, g0_scr)

    @pl.when(e == n_blocks)
    def _():
        g_last = g1_scr if n_blocks % 2 == 0 else g0_scr
        total = acc_ref[...] + jnp.dot(vt_ref[...], g_last[...], preferred_element_type=F32)
        o_ref[...] = x_ref[...] + gate_ref[0] * total.T


def _peer_dense(ht, u, vt, d0, p0, s1, p1z, x2, gate, seq, tb=512, eb=512):
    d, t = ht.shape
    n_exp = u.shape[0]
    nh, nk = PEER_HEADS, PEER_KEYS
    tb, eb = min(tb, seq), min(eb, n_exp)
    ni = eb // nk
    n_blocks = n_exp // eb
    per_batch = seq // tb
    blk = lambda b: jnp.clip(b, 0, n_blocks - 1)
    score_spec = pl.BlockSpec((nh, nk, tb), lambda i, e: (0, 0, i))
    first_spec = pl.BlockSpec((ni, nh, tb), lambda i, e: (blk(e), 0, i))
    return pl.pallas_call(
        functools.partial(_peer_dense_kernel, ni=ni, tb=tb, n_blocks=n_blocks),
        grid=(t // tb, n_blocks + 1),
        in_specs=[pl.BlockSpec((d, tb), lambda i, e: (0, i)),
                  pl.BlockSpec((eb, d), lambda i, e: (blk(e), 0)),
                  pl.BlockSpec((d, eb), lambda i, e: (0, blk(e - 1))),
                  first_spec, first_spec, score_spec, score_spec,
                  pl.BlockSpec((tb, d), lambda i, e: (i, 0), pipeline_mode=pl.Buffered(1)),
                  pl.BlockSpec((1, 1, d), lambda i, e: (i // per_batch, 0, 0))],
        out_specs=pl.BlockSpec((tb, d), lambda i, e: (i, 0)),
        out_shape=jax.ShapeDtypeStruct((t, d), F32),
        scratch_shapes=[pltpu.VMEM((d, tb), F32),
                        pltpu.VMEM((eb, tb), BF16),
                        pltpu.VMEM((eb, tb), BF16),
                        pltpu.VMEM((2 * nk, tb), F32)],
        compiler_params=_params("arbitrary", "arbitrary"),
        name="peer_dense",
    )(ht, u, vt, d0.transpose(1, 0, 2), p0.transpose(1, 0, 2), s1, p1z, x2, gate)


def _peer(x2, g, shift, scale, gate, wq, keys, u, v, seq):
    ht = _normmod(x2, g, shift, scale, seq, transposed=True)
    d0, p0, s1, p1z = _peer_topk(ht, wq.T.astype(BF16), keys.astype(BF16))
    return _peer_dense(ht, u.astype(BF16), v.T.astype(BF16), d0, p0, s1, p1z, x2, gate, seq)


def kernel(x, c, ada_w, ada_b, norm_mix_g, norm_ffn_g, in0_w, conv_w, conv_b, dt_bias, a_log, d_skip,
           ssd_norm_g, gmlp_ln_g, gmlp_ln_b, gmlp_ws, gmlp_bs, out0_w, sb_qkv_w, sb_out_w, peer_wq,
           peer_keys, peer_u, peer_v, final_g):
    bsz, seq, d = x.shape
    depth = ada_w.shape[0]
    xi = SSD_HEADS * SSD_HEAD_DIM
    conv_dim = xi + 2 * SSD_GROUPS * SSD_STATE
    mod = _ada_mod(c, ada_w, ada_b)
    x2 = x.reshape(bsz * seq, d)
    for i in range(depth):
        shift1, scale1, gate1, shift2, scale2, gate2 = (
            m.reshape(bsz, 1, d) for m in jnp.split(mod[i], 6, axis=-1))
        h = _normmod(x2, norm_mix_g[i], shift1, scale1, seq, transposed=False)
        j = i // 2
        if i % 2 == 0:
            w = in0_w[j]
            dt_cols = slice(xi + conv_dim, xi + conv_dim + SSD_HEADS)
            w_main = jnp.concatenate([w[:, :xi + conv_dim], w[:, dt_cols.stop:]], axis=1).astype(BF16)
            w_dt = jnp.pad(w[:, dt_cols], ((0, 0), (0, LANES - SSD_HEADS))).astype(BF16)
            proj = _matmul([h], [w_main], BF16, 1024, 512, name="in0_proj")
            dt_raw = _matmul([h], [w_dt], F32, 1024, LANES, name="in0_dt")
            ya = _ssd(proj, dt_raw, conv_w[j], conv_b[j], dt_bias[j], a_log[j], d_skip[j], ssd_norm_g[j], seq)
            yb = _gmlp(proj, gmlp_ln_g[j], gmlp_ln_b[j], gmlp_ws[j], gmlp_bs[j])
            wo = out0_w[j].astype(BF16)
            x2 = _matmul([ya, yb], [wo[:xi], wo[xi:]], F32, 1024, 512, resid=x2, gate=gate1, seq=seq,
                         name="out0_proj")
        else:
            w_qkv = sb_qkv_w[j]
            n_q = SB_HEADS * SB_HEAD_DIM
            w_qkv = jnp.concatenate([w_qkv[:, :n_q] * SB_HEAD_DIM ** -0.5, w_qkv[:, n_q:]], axis=1).astype(BF16)
            qkv = _matmul([h], [w_qkv], BF16, 1024, 512, name="qkv_proj")
            o = _attn(qkv, seq)
            x2 = _matmul([o], [sb_out_w[j].astype(BF16)], F32, 1024, 512, resid=x2, gate=gate1, seq=seq,
                         name="sb_out_proj")
        x2 = _peer(x2, norm_ffn_g[i], shift2, scale2, gate2, peer_wq[i], peer_keys[i], peer_u[i],
                   peer_v[i], seq)
    return _final_norm(x2, final_g).reshape(bsz, seq, d)
```

```python
import functools

import numpy as np
import jax
import jax.numpy as jnp
from jax import lax
from jax.experimental import pallas as pl
from jax.experimental.pallas import tpu as pltpu

F32 = jnp.float32
BF16 = jnp.bfloat16
EPS = 1e-6
NEG = -1e30
INV_SQRT2 = 0.7071067811865476

SSD_HEADS = 32
SSD_HEAD_DIM = 64
SSD_GROUPS = 8
SSD_STATE = 128
SSD_CONV = 4
SSD_CHUNK = 128
GMLP_GROUPS = 16
GMLP_GROUP_DIM = 128
GMLP_CHUNK = 128
SB_HEADS = 16
SB_HEAD_DIM = 128
PEER_HEADS = 8
PEER_KEYS = 128
PEER_HALF = 128
PEER_TOPK = 16
GATE_ROWS = 128

LANES = 128
SUBLANES = 8
VMEM_LIMIT = 56 * 1024 * 1024


def _params(*sem, vmem=VMEM_LIMIT):
    return pltpu.CompilerParams(dimension_semantics=sem, vmem_limit_bytes=vmem)


def _gelu(x):
    return 0.5 * x * (1.0 + lax.erf(x * INV_SQRT2))


def _silu(x):
    return x * jax.nn.sigmoid(x)


def _softplus(x):
    return jnp.maximum(x, 0.0) + jnp.log1p(jnp.exp(-jnp.abs(x)))


def _split2(v):
    hi = v.astype(BF16)
    lo = (v - hi.astype(F32)).astype(BF16)
    return hi, lo


def _split3(v):
    hi = v.astype(BF16)
    r = v - hi.astype(F32)
    mid = r.astype(BF16)
    lo = (r - mid.astype(F32)).astype(BF16)
    return hi, mid, lo


def _ada_kernel(c_ref, w_ref, b_ref, o_ref):
    cond = _silu(c_ref[...]).astype(BF16)
    o_ref[0] = jnp.dot(cond, w_ref[0].astype(BF16), preferred_element_type=F32) + b_ref[0]


def _ada_mod(c, ada_w, ada_b, tn=512):
    depth, d, n = ada_w.shape
    bsz = c.shape[0]
    rows = -(-bsz // SUBLANES) * SUBLANES
    c_pad = jnp.pad(c, ((0, rows - bsz), (0, 0)))
    out = pl.pallas_call(
        _ada_kernel,
        grid=(depth, n // tn),
        in_specs=[pl.BlockSpec((rows, d), lambda i, j: (0, 0)),
                  pl.BlockSpec((1, d, tn), lambda i, j: (i, 0, j)),
                  pl.BlockSpec((1, 1, tn), lambda i, j: (i, 0, j))],
        out_specs=pl.BlockSpec((1, rows, tn), lambda i, j: (i, 0, j)),
        out_shape=jax.ShapeDtypeStruct((depth, rows, n), F32),
        compiler_params=_params("arbitrary", "arbitrary"),
        name="ada_mod",
    )(c_pad, ada_w, ada_b.reshape(depth, 1, n))
    return out[:, :bsz]


def _normmod_kernel(x_ref, g_ref, sh_ref, sc_ref, o_ref, *, transposed):
    x = x_ref[...]
    y = x * lax.rsqrt(jnp.mean(x * x, axis=-1, keepdims=True) + EPS) * g_ref[...]
    h = y * (1.0 + sc_ref[0]) + sh_ref[0]
    o_ref[...] = (h.T if transposed else h).astype(BF16)


def _normmod(x2, g, shift, scale, seq, transposed, tm=512):
    t, d = x2.shape
    tm = min(tm, seq)
    per_batch = seq // tm
    if transposed:
        out_spec = pl.BlockSpec((d, tm), lambda i: (0, i))
        out_shape = jax.ShapeDtypeStruct((d, t), BF16)
    else:
        out_spec = pl.BlockSpec((tm, d), lambda i: (i, 0))
        out_shape = jax.ShapeDtypeStruct((t, d), BF16)
    return pl.pallas_call(
        functools.partial(_normmod_kernel, transposed=transposed),
        grid=(t // tm,),
        in_specs=[pl.BlockSpec((tm, d), lambda i: (i, 0)),
                  pl.BlockSpec((1, d), lambda i: (0, 0)),
                  pl.BlockSpec((1, 1, d), lambda i: (i // per_batch, 0, 0)),
                  pl.BlockSpec((1, 1, d), lambda i: (i // per_batch, 0, 0))],
        out_specs=out_spec,
        out_shape=out_shape,
        compiler_params=_params("arbitrary"),
        name="normmod_t" if transposed else "normmod",
    )(x2, g.reshape(1, d), shift, scale)


def _final_norm_kernel(x_ref, g_ref, o_ref):
    x = x_ref[...]
    o_ref[...] = x * lax.rsqrt(jnp.mean(x * x, axis=-1, keepdims=True) + EPS) * g_ref[...]


def _final_norm(x2, g, tm=512):
    t, d = x2.shape
    tm = min(tm, t)
    return pl.pallas_call(
        _final_norm_kernel,
        grid=(t // tm,),
        in_specs=[pl.BlockSpec((tm, d), lambda i: (i, 0)),
                  pl.BlockSpec((1, d), lambda i: (0, 0))],
        out_specs=pl.BlockSpec((tm, d), lambda i: (i, 0)),
        out_shape=jax.ShapeDtypeStruct((t, d), F32),
        compiler_params=_params("arbitrary"),
        name="final_norm",
    )(x2, g.reshape(1, d))


def _mm_kernel(*refs, n_pairs, has_resid):
    acc = None
    for a_ref, b_ref in zip(refs[:n_pairs], refs[n_pairs:2 * n_pairs]):
        part = jnp.dot(a_ref[...], b_ref[...], preferred_element_type=F32)
        acc = part if acc is None else acc + part
    if has_resid:
        x_ref, gate_ref, o_ref = refs[2 * n_pairs:]
        o_ref[...] = x_ref[...] + gate_ref[0] * acc
    else:
        o_ref = refs[2 * n_pairs]
        o_ref[...] = acc.astype(o_ref.dtype)


def _matmul(a_list, b_list, out_dtype, tm, tn, resid=None, gate=None, seq=None, name="matmul"):
    m = a_list[0].shape[0]
    n = b_list[0].shape[1]
    tm, tn = min(tm, m if seq is None else seq), min(tn, n)
    in_specs = [pl.BlockSpec((tm, a.shape[1]), lambda i, j: (i, 0)) for a in a_list]
    in_specs += [pl.BlockSpec((b.shape[0], tn), lambda i, j: (0, j)) for b in b_list]
    args = list(a_list) + list(b_list)
    if resid is not None:
        per_batch = seq // tm
        in_specs += [pl.BlockSpec((tm, tn), lambda i, j: (i, j)),
                     pl.BlockSpec((1, 1, tn), lambda i, j: (i // per_batch, 0, j))]
        args += [resid, gate]
    return pl.pallas_call(
        functools.partial(_mm_kernel, n_pairs=len(a_list), has_resid=resid is not None),
        grid=(m // tm, n // tn),
        in_specs=in_specs,
        out_specs=pl.BlockSpec((tm, tn), lambda i, j: (i, j)),
        out_shape=jax.ShapeDtypeStruct((m, n), out_dtype),
        compiler_params=_params("arbitrary", "arbitrary"),
        name=name,
    )(*args)


def _ssd_kernel(z_ref, xs_ref, bc_ref, dt_ref, cw_ref, cb_ref, dtb_ref, alog_ref, dskip_ref, ng_ref,
                expand_ref, o_ref, xbuf, bcbuf, ybuf, state_ref):
    L, H, P, G, N = SSD_CHUNK, SSD_HEADS, SSD_HEAD_DIM, SSD_GROUPS, SSD_STATE
    R = H // G
    XI = H * P
    HALO = SUBLANES

    @pl.when(pl.program_id(1) == 0)
    def _():
        xbuf[0:HALO, :] = jnp.zeros((HALO, XI), F32)
        bcbuf[0:HALO, :] = jnp.zeros((HALO, 2 * G * N), F32)
        state_ref[...] = jnp.zeros(state_ref.shape, F32)

    xbuf[HALO:HALO + L, :] = xs_ref[...].astype(F32)
    bcbuf[HALO:HALO + L, :] = bc_ref[...].astype(F32)

    def conv_silu(buf, w, b):
        acc = b
        for k in range(SSD_CONV):
            off = HALO - (SSD_CONV - 1) + k
            acc = acc + w[k:k + 1, :] * buf[off:off + L, :]
        return _silu(acc)

    cw = cw_ref[...]
    cb = cb_ref[...]
    xs = conv_silu(xbuf, cw[:, :XI], cb[:, :XI])
    bc = conv_silu(bcbuf, cw[:, XI:], cb[:, XI:])
    xbuf[0:HALO, :] = xbuf[L:L + HALO, :]
    bcbuf[0:HALO, :] = bcbuf[L:L + HALO, :]

    dt = _softplus(dt_ref[...] + dtb_ref[...])
    a = dt * (-jnp.exp(alog_ref[...]))
    row = lax.broadcasted_iota(jnp.int32, (L, L), 0)
    col = lax.broadcasted_iota(jnp.int32, (L, L), 1)
    causal = col <= row
    tri = jnp.where(causal, 1.0, 0.0).astype(BF16)
    a_cum = sum(jnp.dot(tri, part, preferred_element_type=F32) for part in _split3(a))
    a_cum_t = a_cum.T
    a_last = a_cum[L - 1:L, :]
    dte = jnp.exp(a_last - a_cum)
    eac = jnp.exp(a_cum)

    stk = jnp.concatenate([dt, dt * dte, eac], axis=0)
    hi, lo = _split2(stk)
    ex = jnp.dot(jnp.concatenate([hi, lo], axis=1), expand_ref[...], preferred_element_type=F32)
    dt_e, dd_e, eac_e = ex[0:L], ex[L:2 * L], ex[2 * L:3 * L]
    x_in = xs * dt_e
    xd = (xs * dd_e).astype(BF16)
    cdec_e = eac_e[L - 1:L, :]

    GW = R * P
    lane_head = lax.broadcasted_iota(jnp.int32, (L, GW), 1) // P
    for g in range(G):
        lanes = slice(g * GW, (g + 1) * GW)
        bg = bc[:, g * N:(g + 1) * N]
        cg = bc[:, G * N + g * N:G * N + (g + 1) * N].astype(BF16)
        cbm = lax.dot_general(cg, bg.astype(BF16), (((1,), (1,)), ((), ())), preferred_element_type=F32)
        xg = x_in[:, lanes]
        ms, xblocks = [], []
        for r in range(R):
            h = g * R + r
            seg = a_cum[:, h:h + 1] - a_cum_t[h:h + 1, :]
            dec = jnp.where(causal, jnp.exp(jnp.where(causal, seg, 0.0)), 0.0)
            ms.append((cbm * dec).astype(BF16))
            xblocks.append(jnp.where(lane_head == r, xg, 0.0).astype(BF16))
        y_diag = jnp.dot(jnp.concatenate(ms, axis=1), jnp.concatenate(xblocks, axis=0),
                         preferred_element_type=F32)
        hg = state_ref[g]
        y_off = jnp.dot(cg, hg.astype(BF16), preferred_element_type=F32) * eac_e[:, lanes]
        new_states = jnp.dot(bg.T.astype(BF16), xd[:, lanes], preferred_element_type=F32)
        state_ref[g] = hg * cdec_e[:, lanes] + new_states
        ybuf[:, lanes] = y_diag + y_off + dskip_ref[:, lanes] * xs[:, lanes]

    z = z_ref[...].astype(F32)
    yz = ybuf[...] * _silu(z)
    ya = yz * lax.rsqrt(jnp.mean(yz * yz, axis=-1, keepdims=True) + EPS) * ng_ref[...]
    o_ref[...] = ya.astype(BF16)


def _ssd(proj, dt_raw, conv_w, conv_b, dt_bias, a_log, d_skip, norm_g, seq):
    t = proj.shape[0]
    L, H, P, G, N = SSD_CHUNK, SSD_HEADS, SSD_HEAD_DIM, SSD_GROUPS, SSD_STATE
    xi = H * P
    bcw = 2 * G * N
    nc = seq // L
    pad = LANES - H
    expand = np.zeros((LANES, xi), np.float32)
    expand[np.arange(xi) // P, np.arange(xi)] = 1.0
    expand2 = jnp.asarray(np.concatenate([expand, expand], axis=0), BF16)
    row = lambda v: jnp.pad(v.astype(F32), (0, pad)).reshape(1, LANES)
    full = lambda shape: pl.BlockSpec(shape, lambda b, c: (0, 0))
    return pl.pallas_call(
        _ssd_kernel,
        grid=(t // seq, nc),
        in_specs=[pl.BlockSpec((L, xi), lambda b, c: (b * nc + c, 0)),
                  pl.BlockSpec((L, xi), lambda b, c: (b * nc + c, 1)),
                  pl.BlockSpec((L, bcw), lambda b, c: (b * nc + c, 2)),
                  pl.BlockSpec((L, LANES), lambda b, c: (b * nc + c, 0)),
                  full((SSD_CONV, xi + bcw)), full((1, xi + bcw)),
                  full((1, LANES)), full((1, LANES)), full((1, xi)), full((1, xi)),
                  full((2 * LANES, xi))],
        out_specs=pl.BlockSpec((L, xi), lambda b, c: (b * nc + c, 0)),
        out_shape=jax.ShapeDtypeStruct((t, xi), BF16),
        scratch_shapes=[pltpu.VMEM((SUBLANES + L, xi), F32),
                        pltpu.VMEM((SUBLANES + L, bcw), F32),
                        pltpu.VMEM((L, xi), F32),
                        pltpu.VMEM((G, N, (H // G) * P), F32)],
        compiler_params=_params("arbitrary", "arbitrary"),
        name="ssd",
    )(proj, proj, proj, dt_raw, conv_w, conv_b.reshape(1, -1), row(dt_bias), row(a_log),
      jnp.repeat(d_skip.astype(F32), P).reshape(1, xi), norm_g.reshape(1, xi), expand2)


def _gmlp_kernel(u_ref, v_ref, lng_ref, lnb_ref, ws_ref, bs_ref, o_ref, *, nchunk):
    L, NG, GD = GMLP_CHUNK, GMLP_GROUPS, GMLP_GROUP_DIM
    row = lax.broadcasted_iota(jnp.int32, (L, L), 0)
    col = lax.broadcasted_iota(jnp.int32, (L, L), 1)
    tril = col <= row
    for ch in range(nchunk):
        rows = slice(ch * L, (ch + 1) * L)
        ug = _gelu(u_ref[rows, :].astype(F32))
        vg = _gelu(v_ref[rows, :].astype(F32))
        mu = jnp.mean(vg, axis=-1, keepdims=True)
        xc = vg - mu
        var = jnp.mean(xc * xc, axis=-1, keepdims=True)
        vn = (xc * lax.rsqrt(var + EPS) * lng_ref[...] + lnb_ref[...]).astype(BF16)
        for g in range(NG):
            cols = slice(g * GD, (g + 1) * GD)
            w = jnp.where(tril, ws_ref[g], 0.0).astype(BF16)
            vv = jnp.dot(w, vn[:, cols], preferred_element_type=F32) + bs_ref[:, cols]
            o_ref[rows, cols] = (ug[:, cols] * vv).astype(BF16)


def _gmlp(proj, ln_g, ln_b, ws, bs, nchunk=2):
    t = proj.shape[0]
    L, NG, GD = GMLP_CHUNK, GMLP_GROUPS, GMLP_GROUP_DIM
    w = NG * GD
    tm = nchunk * L
    bs_full = jnp.repeat(bs.T.astype(F32), GD, axis=1)
    return pl.pallas_call(
        functools.partial(_gmlp_kernel, nchunk=nchunk),
        grid=(t // tm,),
        in_specs=[pl.BlockSpec((tm, w), lambda i: (i, 3)),
                  pl.BlockSpec((tm, w), lambda i: (i, 4)),
                  pl.BlockSpec((1, w), lambda i: (0, 0)),
                  pl.BlockSpec((1, w), lambda i: (0, 0)),
                  pl.BlockSpec((NG, L, L), lambda i: (0, 0, 0)),
                  pl.BlockSpec((L, w), lambda i: (0, 0))],
        out_specs=pl.BlockSpec((tm, w), lambda i: (i, 0)),
        out_shape=jax.ShapeDtypeStruct((t, w), BF16),
        compiler_params=_params("arbitrary"),
        name="gmlp",
    )(proj, proj, ln_g.reshape(1, w), ln_b.reshape(1, w), ws, bs_full)


def _attn_kernel(q_ref, k_ref, v_ref, o_ref, *, tq, strip):
    i = pl.program_id(2)
    tk = tq // 2
    n_strips = tq // strip
    r2 = lax.broadcasted_iota(jnp.int32, (tk, tk), 0)
    c2 = lax.broadcasted_iota(jnp.int32, (tk, tk), 1)
    after = jnp.where(r2 >= c2, 1.0, 0.0).astype(BF16)
    row = lax.broadcasted_iota(jnp.int32, (strip, tk), 0)
    col = lax.broadcasted_iota(jnp.int32, (strip, tk), 1)

    def tiles(js, carry, diag):
        kbs = [k_ref[pl.ds(pl.multiple_of(j * tk, tk), tk), :] for j in js]
        vbs = [v_ref[pl.ds(pl.multiple_of(j * tk, tk), tk), :] for j in js]
        chains, masks = [], {}
        for s in range(n_strips):
            for b in range(len(js)):
                shift = s * strip - (len(js) - 1 - b) * tk if diag else tk
                if shift + strip - 1 <= 0:
                    continue
                chains.append((s, b))
                masks[(s, b)] = None if shift >= tk else (col < row + shift)
        logits = {(s, b): lax.dot_general(q_ref[s * strip:(s + 1) * strip, :], kbs[b], (((1,), (1,)), ((), ())),
                                          preferred_element_type=F32) for s, b in chains}
        log_1m, split = {}, {}
        for c in chains:
            m = -(jnp.maximum(logits[c], 0.0) + jnp.log(1.0 + jnp.exp(-jnp.abs(logits[c]))))
            if masks[c] is not None:
                m = jnp.where(masks[c], m, 0.0)
            log_1m[c] = m
            split[c] = m.astype(BF16)
        tail = {c: jnp.dot(split[c], after, preferred_element_type=F32) for c in chains}
        runs = {}
        new_run = []
        for s in range(n_strips):
            run = carry[2 * s + 1]
            for b in range(len(js)):
                if (s, b) in masks:
                    runs[(s, b)] = run
                    run = run + jnp.sum(log_1m[(s, b)], axis=-1, keepdims=True)
            new_run.append(run)
        weights = {}
        for c in chains:
            w = jnp.exp(logits[c] + tail[c] + runs[c])
            if masks[c] is not None:
                w = jnp.where(masks[c], w, 0.0)
            weights[c] = w.astype(BF16)
        pv = {c: jnp.dot(weights[c], vbs[c[1]], preferred_element_type=F32) for c in chains}
        out = []
        for s in range(n_strips):
            acc = carry[2 * s]
            for b in range(len(js)):
                if (s, b) in pv:
                    acc = acc + pv[(s, b)]
            out.extend([acc, new_run[s]])
        return tuple(out)

    init = tuple(jnp.zeros((strip, SB_HEAD_DIM if k % 2 == 0 else 1), F32) for k in range(2 * n_strips))
    carry = tiles([2 * i + 1, 2 * i], init, True)
    carry = lax.fori_loop(0, i, lambda p, c: tiles([2 * (i - p) - 1, 2 * (i - p) - 2], c, False), carry)
    for s in range(n_strips):
        o_ref[s * strip:(s + 1) * strip, :] = carry[2 * s].astype(BF16)


def _attn(qkv, seq, tq=512, strip=128):
    t = qkv.shape[0]
    hd, nh = SB_HEAD_DIM, SB_HEADS
    tq = min(tq, seq)
    nq = seq // tq
    return pl.pallas_call(
        functools.partial(_attn_kernel, tq=tq, strip=min(strip, tq)),
        grid=(t // seq, nh, nq),
        in_specs=[pl.BlockSpec((tq, hd), lambda b, h, i: (b * nq + i, h)),
                  pl.BlockSpec((seq, hd), lambda b, h, i: (b, nh + h)),
                  pl.BlockSpec((seq, hd), lambda b, h, i: (b, 2 * nh + h))],
        out_specs=pl.BlockSpec((tq, hd), lambda b, h, i: (b * nq + i, h)),
        out_shape=jax.ShapeDtypeStruct((t, nh * hd), BF16),
        compiler_params=_params("arbitrary", "arbitrary", "arbitrary"),
        name="sb_attn",
    )(qkv, qkv, qkv)


RANK_SCALE = 2.0 ** 100
RANK_NONE = 64.0


def _extract_top(work, count):
    tops = []
    for k in range(count):
        m = jnp.max(work, axis=0, keepdims=True)
        tops.append(m)
        work = jnp.where(work == m, -(2.0 - k / 32.0) * RANK_SCALE, work)
    return tops, RANK_NONE + work * (32.0 / RANK_SCALE)


def _peer_topk_kernel(ht_ref, wqt_ref, keys_ref, l0_ref, p0_ref, r1_ref, p1z_ref, q_scr, cand_scr):
    nh, kd, topk = PEER_HEADS, PEER_HALF, PEER_TOPK
    n_cand = sum((topk + 1) // (k + 1) for k in range(topk + 1))
    q_scr[...] = jnp.dot(wqt_ref[...], ht_ref[...], preferred_element_type=F32).astype(BF16)
    cand_scr[...] = jnp.full(cand_scr.shape, NEG, F32)

    def head(h, carry):
        scores, tops, ranks = [], [], []
        for half in range(2):
            start = pl.multiple_of(h * (2 * kd) + half * kd, kd)
            sc = jnp.dot(keys_ref[h, half], q_scr[pl.ds(start, kd), :], preferred_element_type=F32)
            scores.append(sc)
            top, rank = _extract_top(sc, topk + 1)
            tops.append(top)
            ranks.append(rank)
        r = 0
        for k in range(topk + 1):
            for l in range((topk + 1) // (k + 1)):
                cand_scr[pl.ds(r, 1), :] = tops[0][k] + tops[1][l]
                r += 1
        cand = cand_scr[...]
        best, _ = _extract_top(cand, topk + 1)
        theta = 0.5 * (best[topk - 1] + best[topk])
        zsum = jnp.sum(jnp.where(cand > theta, jnp.exp(cand - best[0]), 0.0), axis=0, keepdims=True)
        margin = scores[0] - theta
        l0 = None
        for l in range(topk + 1):
            hit = jnp.clip((margin + tops[1][l]) * RANK_SCALE, 0.0, 1.0)
            l0 = hit if l0 is None else l0 + hit
        l0_ref[h] = l0
        p0_ref[h] = jnp.exp(scores[0] - tops[0][0])
        r1_ref[h] = ranks[1].astype(BF16)
        p1z_ref[h] = (jnp.exp(scores[1] - tops[1][0]) / zsum).astype(BF16)
        return carry

    assert n_cand <= cand_scr.shape[0]
    lax.fori_loop(0, nh, head, 0)


def _peer_topk(ht, wqt, keys, tb=256):
    d, t = ht.shape
    nh, nk, kd = PEER_HEADS, PEER_KEYS, PEER_HALF
    n_cand = sum((PEER_TOPK + 1) // (k + 1) for k in range(PEER_TOPK + 1))
    cand_rows = -(-n_cand // SUBLANES) * SUBLANES
    out_spec = pl.BlockSpec((nh, nk, tb), lambda i: (0, 0, i))
    out_shape = jax.ShapeDtypeStruct((nh, nk, t), F32)
    return pl.pallas_call(
        _peer_topk_kernel,
        grid=(t // tb,),
        in_specs=[pl.BlockSpec((d, tb), lambda i: (0, i)),
                  pl.BlockSpec((nh * 2 * kd, d), lambda i: (0, 0)),
                  pl.BlockSpec((nh, 2, nk, kd), lambda i: (0, 0, 0, 0))],
        out_specs=[out_spec] * 4,
        out_shape=[out_shape, out_shape, jax.ShapeDtypeStruct((nh, nk, t), BF16),
                   jax.ShapeDtypeStruct((nh, nk, t), BF16)],
        scratch_shapes=[pltpu.VMEM((nh * 2 * kd, tb), BF16),
                        pltpu.VMEM((cand_rows, tb), F32)],
        compiler_params=_params("arbitrary"),
        name="peer_topk",
    )(ht, wqt, keys)


def _peer_dense_kernel(ht_ref, u_ref, vt_ref, l0_ref, p0_ref, r1_ref, p1z_ref, x_ref, gate_ref, o_ref,
                       acc_ref, g0_scr, g1_scr, a_scr, *, ni, tb, n_blocks):
    nh, nk = PEER_HEADS, PEER_KEYS
    d = acc_ref.shape[0]
    e = pl.program_id(1)
    pair = 2 * nk
    n_chunks = ni * nk // pair
    acc_rows = d // n_chunks

    @pl.when(e == 0)
    def _():
        acc_ref[...] = jnp.zeros(acc_ref.shape, F32)
        g1_scr[...] = jnp.zeros(g1_scr.shape, BF16)

    def gate_chunk(c, g_cur):
        for sub in range(pair // nk):
            ii = c * (pair // nk) + sub
            for cc in range(tb // LANES):
                cs = slice(cc * LANES, (cc + 1) * LANES)
                for j0 in range(0, nk, GATE_ROWS):
                    js = slice(j0, j0 + GATE_ROWS)
                    w = None
                    for h in range(nh):
                        l0 = l0_ref[ii, h:h + 1, cs].astype(BF16)
                        p0 = p0_ref[ii, h:h + 1, cs].astype(BF16)
                        hit = jnp.clip(l0 - r1_ref[h, js, cs], 0.0, 1.0)
                        term = hit * p1z_ref[h, js, cs] * p0
                        w = term if w is None else w + term
                    a = a_scr[sub * nk + j0:sub * nk + j0 + GATE_ROWS, cs]
                    g_cur[ii * nk + j0:ii * nk + j0 + GATE_ROWS, cs] = _gelu(a).astype(BF16) * w

    def first_product(c):
        a_scr[...] = jnp.dot(u_ref[c * pair:(c + 1) * pair, :], ht_ref[...], preferred_element_type=F32)

    def step(g_cur, g_prev):
        for c in range(n_chunks):
            first_product(c)
            rows = slice(c * acc_rows, (c + 1) * acc_rows)
            acc_ref[rows, :] += jnp.dot(vt_ref[rows, :], g_prev[...], preferred_element_type=F32)
            gate_chunk(c, g_cur)

    @pl.when(jnp.logical_and(e < n_blocks, e % 2 == 0))
    def _():
        step(g0_scr, g1_scr)

    @pl.when(jnp.logical_and(e < n_blocks, e % 2 == 1))
    def _():
        step(g1_scr, g0_scr)

    @pl.when(e == n_blocks)
    def _():
        g_last = g1_scr if n_blocks % 2 == 0 else g0_scr
        total = acc_ref[...] + jnp.dot(vt_ref[...], g_last[...], preferred_element_type=F32)
        o_ref[...] = x_ref[...] + gate_ref[0] * total.T


def _peer_dense(ht, u, vt, l0, p0, r1, p1z, x2, gate, seq, tb=512, eb=512):
    d, t = ht.shape
    n_exp = u.shape[0]
    nh, nk = PEER_HEADS, PEER_KEYS
    tb, eb = min(tb, seq), min(eb, n_exp)
    ni = eb // nk
    n_blocks = n_exp // eb
    per_batch = seq // tb
    blk = lambda b: jnp.clip(b, 0, n_blocks - 1)
    score_spec = pl.BlockSpec((nh, nk, tb), lambda i, e: (0, 0, i))
    first_spec = pl.BlockSpec((ni, nh, tb), lambda i, e: (blk(e), 0, i))
    return pl.pallas_call(
        functools.partial(_peer_dense_kernel, ni=ni, tb=tb, n_blocks=n_blocks),
        grid=(t // tb, n_blocks + 1),
        in_specs=[pl.BlockSpec((d, tb), lambda i, e: (0, i)),
                  pl.BlockSpec((eb, d), lambda i, e: (blk(e), 0)),
                  pl.BlockSpec((d, eb), lambda i, e: (0, blk(e - 1))),
                  first_spec, first_spec, score_spec, score_spec,
                  pl.BlockSpec((tb, d), lambda i, e: (i, 0), pipeline_mode=pl.Buffered(1)),
                  pl.BlockSpec((1, 1, d), lambda i, e: (i // per_batch, 0, 0))],
        out_specs=pl.BlockSpec((tb, d), lambda i, e: (i, 0)),
        out_shape=jax.ShapeDtypeStruct((t, d), F32),
        scratch_shapes=[pltpu.VMEM((d, tb), F32),
                        pltpu.VMEM((eb, tb), BF16),
                        pltpu.VMEM((eb, tb), BF16),
                        pltpu.VMEM((2 * nk, tb), F32)],
        compiler_params=_params("arbitrary", "arbitrary"),
        name="peer_dense",
    )(ht, u, vt, l0.transpose(1, 0, 2), p0.transpose(1, 0, 2), r1, p1z, x2, gate)


def _peer(x2, g, shift, scale, gate, wq, keys, u, v, seq):
    ht = _normmod(x2, g, shift, scale, seq, transposed=True)
    l0, p0, r1, p1z = _peer_topk(ht, wq.T.astype(BF16), keys.astype(BF16))
    return _peer_dense(ht, u.astype(BF16), v.T.astype(BF16), l0, p0, r1, p1z, x2, gate, seq)


def kernel(x, c, ada_w, ada_b, norm_mix_g, norm_ffn_g, in0_w, conv_w, conv_b, dt_bias, a_log, d_skip,
           ssd_norm_g, gmlp_ln_g, gmlp_ln_b, gmlp_ws, gmlp_bs, out0_w, sb_qkv_w, sb_out_w, peer_wq,
           peer_keys, peer_u, peer_v, final_g):
    bsz, seq, d = x.shape
    depth = ada_w.shape[0]
    xi = SSD_HEADS * SSD_HEAD_DIM
    conv_dim = xi + 2 * SSD_GROUPS * SSD_STATE
    mod = _ada_mod(c, ada_w, ada_b)
    x2 = x.reshape(bsz * seq, d)
    for i in range(depth):
        shift1, scale1, gate1, shift2, scale2, gate2 = (
            m.reshape(bsz, 1, d) for m in jnp.split(mod[i], 6, axis=-1))
        h = _normmod(x2, norm_mix_g[i], shift1, scale1, seq, transposed=False)
        j = i // 2
        if i % 2 == 0:
            w = in0_w[j]
            dt_cols = slice(xi + conv_dim, xi + conv_dim + SSD_HEADS)
            w_main = jnp.concatenate([w[:, :xi + conv_dim], w[:, dt_cols.stop:]], axis=1).astype(BF16)
            w_dt = jnp.pad(w[:, dt_cols], ((0, 0), (0, LANES - SSD_HEADS))).astype(BF16)
            proj = _matmul([h], [w_main], BF16, 1024, 512, name="in0_proj")
            dt_raw = _matmul([h], [w_dt], F32, 1024, LANES, name="in0_dt")
            ya = _ssd(proj, dt_raw, conv_w[j], conv_b[j], dt_bias[j], a_log[j], d_skip[j], ssd_norm_g[j], seq)
            yb = _gmlp(proj, gmlp_ln_g[j], gmlp_ln_b[j], gmlp_ws[j], gmlp_bs[j])
            wo = out0_w[j].astype(BF16)
            x2 = _matmul([ya, yb], [wo[:xi], wo[xi:]], F32, 1024, 512, resid=x2, gate=gate1, seq=seq,
                         name="out0_proj")
        else:
            w_qkv = sb_qkv_w[j]
            n_q = SB_HEADS * SB_HEAD_DIM
            w_qkv = jnp.concatenate([w_qkv[:, :n_q] * SB_HEAD_DIM ** -0.5, w_qkv[:, n_q:]], axis=1).astype(BF16)
            qkv = _matmul([h], [w_qkv], BF16, 1024, 512, name="qkv_proj")
            o = _attn(qkv, seq)
            x2 = _matmul([o], [sb_out_w[j].astype(BF16)], F32, 1024, 512, resid=x2, gate=gate1, seq=seq,
                         name="sb_out_proj")
        x2 = _peer(x2, norm_ffn_g[i], shift2, scale2, gate2, peer_wq[i], peer_keys[i], peer_u[i],
                   peer_v[i], seq)
    return _final_norm(x2, final_g).reshape(bsz, seq, d)
```

```python
import functools

import numpy as np
import jax
import jax.numpy as jnp
from jax import lax
from jax.experimental import pallas as pl
from jax.experimental.pallas import tpu as pltpu

F32 = jnp.float32
BF16 = jnp.bfloat16
EPS = 1e-6
NEG = -1e30
INV_SQRT2 = 0.7071067811865476

SSD_HEADS = 32
SSD_HEAD_DIM = 64
SSD_GROUPS = 8
SSD_STATE = 128
SSD_CONV = 4
SSD_CHUNK = 128
GMLP_GROUPS = 16
GMLP_GROUP_DIM = 128
GMLP_CHUNK = 128
SB_HEADS = 16
SB_HEAD_DIM = 128
PEER_HEADS = 8
PEER_KEYS = 128
PEER_HALF = 128
PEER_TOPK = 16
GATE_ROWS = 128

LANES = 128
SUBLANES = 8
VMEM_LIMIT = 56 * 1024 * 1024


def _params(*sem, vmem=VMEM_LIMIT):
    return pltpu.CompilerParams(dimension_semantics=sem, vmem_limit_bytes=vmem)


def _gelu(x):
    return 0.5 * x * (1.0 + lax.erf(x * INV_SQRT2))


def _silu(x):
    return x * jax.nn.sigmoid(x)


def _softplus(x):
    return jnp.maximum(x, 0.0) + jnp.log1p(jnp.exp(-jnp.abs(x)))


def _split2(v):
    hi = v.astype(BF16)
    lo = (v - hi.astype(F32)).astype(BF16)
    return hi, lo


def _split3(v):
    hi = v.astype(BF16)
    r = v - hi.astype(F32)
    mid = r.astype(BF16)
    lo = (r - mid.astype(F32)).astype(BF16)
    return hi, mid, lo


def _ada_kernel(c_ref, w_ref, b_ref, o_ref):
    cond = _silu(c_ref[...]).astype(BF16)
    o_ref[0] = jnp.dot(cond, w_ref[0].astype(BF16), preferred_element_type=F32) + b_ref[0]


def _ada_mod(c, ada_w, ada_b, tn=512):
    depth, d, n = ada_w.shape
    bsz = c.shape[0]
    rows = -(-bsz // SUBLANES) * SUBLANES
    c_pad = jnp.pad(c, ((0, rows - bsz), (0, 0)))
    out = pl.pallas_call(
        _ada_kernel,
        grid=(depth, n // tn),
        in_specs=[pl.BlockSpec((rows, d), lambda i, j: (0, 0)),
                  pl.BlockSpec((1, d, tn), lambda i, j: (i, 0, j)),
                  pl.BlockSpec((1, 1, tn), lambda i, j: (i, 0, j))],
        out_specs=pl.BlockSpec((1, rows, tn), lambda i, j: (i, 0, j)),
        out_shape=jax.ShapeDtypeStruct((depth, rows, n), F32),
        compiler_params=_params("arbitrary", "arbitrary"),
        name="ada_mod",
    )(c_pad, ada_w, ada_b.reshape(depth, 1, n))
    return out[:, :bsz]


def _normmod_kernel(x_ref, g_ref, sh_ref, sc_ref, o_ref, *, transposed):
    x = x_ref[...]
    y = x * lax.rsqrt(jnp.mean(x * x, axis=-1, keepdims=True) + EPS) * g_ref[...]
    h = y * (1.0 + sc_ref[0]) + sh_ref[0]
    o_ref[...] = (h.T if transposed else h).astype(BF16)


def _normmod(x2, g, shift, scale, seq, transposed, tm=512):
    t, d = x2.shape
    tm = min(tm, seq)
    per_batch = seq // tm
    if transposed:
        out_spec = pl.BlockSpec((d, tm), lambda i: (0, i))
        out_shape = jax.ShapeDtypeStruct((d, t), BF16)
    else:
        out_spec = pl.BlockSpec((tm, d), lambda i: (i, 0))
        out_shape = jax.ShapeDtypeStruct((t, d), BF16)
    return pl.pallas_call(
        functools.partial(_normmod_kernel, transposed=transposed),
        grid=(t // tm,),
        in_specs=[pl.BlockSpec((tm, d), lambda i: (i, 0)),
                  pl.BlockSpec((1, d), lambda i: (0, 0)),
                  pl.BlockSpec((1, 1, d), lambda i: (i // per_batch, 0, 0)),
                  pl.BlockSpec((1, 1, d), lambda i: (i // per_batch, 0, 0))],
        out_specs=out_spec,
        out_shape=out_shape,
        compiler_params=_params("arbitrary"),
        name="normmod_t" if transposed else "normmod",
    )(x2, g.reshape(1, d), shift, scale)


def _mm_kernel(*refs, n_pairs, has_resid):
    acc = None
    for a_ref, b_ref in zip(refs[:n_pairs], refs[n_pairs:2 * n_pairs]):
        part = jnp.dot(a_ref[...], b_ref[...], preferred_element_type=F32)
        acc = part if acc is None else acc + part
    if has_resid:
        x_ref, gate_ref, o_ref = refs[2 * n_pairs:]
        o_ref[...] = x_ref[...] + gate_ref[0] * acc
    else:
        o_ref = refs[2 * n_pairs]
        o_ref[...] = acc.astype(o_ref.dtype)


def _matmul(a_list, b_list, out_dtype, tm, tn, resid=None, gate=None, seq=None, name="matmul"):
    m = a_list[0].shape[0]
    n = b_list[0].shape[1]
    tm, tn = min(tm, m if seq is None else seq), min(tn, n)
    in_specs = [pl.BlockSpec((tm, a.shape[1]), lambda i, j: (i, 0)) for a in a_list]
    in_specs += [pl.BlockSpec((b.shape[0], tn), lambda i, j: (0, j)) for b in b_list]
    args = list(a_list) + list(b_list)
    if resid is not None:
        per_batch = seq // tm
        in_specs += [pl.BlockSpec((tm, tn), lambda i, j: (i, j)),
                     pl.BlockSpec((1, 1, tn), lambda i, j: (i // per_batch, 0, j))]
        args += [resid, gate]
    return pl.pallas_call(
        functools.partial(_mm_kernel, n_pairs=len(a_list), has_resid=resid is not None),
        grid=(m // tm, n // tn),
        in_specs=in_specs,
        out_specs=pl.BlockSpec((tm, tn), lambda i, j: (i, j)),
        out_shape=jax.ShapeDtypeStruct((m, n), out_dtype),
        compiler_params=_params("arbitrary", "arbitrary"),
        name=name,
    )(*args)


def _ssd_kernel(z_ref, xs_ref, bc_ref, dt_ref, cw_ref, cb_ref, dtb_ref, alog_ref, dskip_ref, ng_ref,
                expand_ref, o_ref, xbuf, bcbuf, ybuf, state_ref):
    L, H, P, G, N = SSD_CHUNK, SSD_HEADS, SSD_HEAD_DIM, SSD_GROUPS, SSD_STATE
    R = H // G
    XI = H * P
    HALO = SUBLANES

    @pl.when(pl.program_id(1) == 0)
    def _():
        xbuf[0:HALO, :] = jnp.zeros((HALO, XI), F32)
        bcbuf[0:HALO, :] = jnp.zeros((HALO, 2 * G * N), F32)
        state_ref[...] = jnp.zeros(state_ref.shape, F32)

    xbuf[HALO:HALO + L, :] = xs_ref[...].astype(F32)
    bcbuf[HALO:HALO + L, :] = bc_ref[...].astype(F32)

    def conv_silu(buf, w, b):
        acc = b
        for k in range(SSD_CONV):
            off = HALO - (SSD_CONV - 1) + k
            acc = acc + w[k:k + 1, :] * buf[off:off + L, :]
        return _silu(acc)

    cw = cw_ref[...]
    cb = cb_ref[...]
    xs = conv_silu(xbuf, cw[:, :XI], cb[:, :XI])
    bc = conv_silu(bcbuf, cw[:, XI:], cb[:, XI:])
    xbuf[0:HALO, :] = xbuf[L:L + HALO, :]
    bcbuf[0:HALO, :] = bcbuf[L:L + HALO, :]

    dt = _softplus(dt_ref[...] + dtb_ref[...])
    a = dt * (-jnp.exp(alog_ref[...]))
    row = lax.broadcasted_iota(jnp.int32, (L, L), 0)
    col = lax.broadcasted_iota(jnp.int32, (L, L), 1)
    causal = col <= row
    tri = jnp.where(causal, 1.0, 0.0).astype(BF16)
    a_cum = sum(jnp.dot(tri, part, preferred_element_type=F32) for part in _split3(a))
    a_cum_t = a_cum.T
    a_last = a_cum[L - 1:L, :]
    dte = jnp.exp(a_last - a_cum)
    eac = jnp.exp(a_cum)

    stk = jnp.concatenate([dt, dt * dte, eac], axis=0)
    hi, lo = _split2(stk)
    ex = jnp.dot(jnp.concatenate([hi, lo], axis=1), expand_ref[...], preferred_element_type=F32)
    dt_e, dd_e, eac_e = ex[0:L], ex[L:2 * L], ex[2 * L:3 * L]
    x_in = xs * dt_e
    xd = (xs * dd_e).astype(BF16)
    cdec_e = eac_e[L - 1:L, :]

    GW = R * P
    lane_head = lax.broadcasted_iota(jnp.int32, (L, GW), 1) // P
    for g in range(G):
        lanes = slice(g * GW, (g + 1) * GW)
        bg = bc[:, g * N:(g + 1) * N]
        cg = bc[:, G * N + g * N:G * N + (g + 1) * N].astype(BF16)
        cbm = lax.dot_general(cg, bg.astype(BF16), (((1,), (1,)), ((), ())), preferred_element_type=F32)
        xg = x_in[:, lanes]
        ms, xblocks = [], []
        for r in range(R):
            h = g * R + r
            seg = a_cum[:, h:h + 1] - a_cum_t[h:h + 1, :]
            dec = jnp.where(causal, jnp.exp(jnp.where(causal, seg, 0.0)), 0.0)
            ms.append((cbm * dec).astype(BF16))
            xblocks.append(jnp.where(lane_head == r, xg, 0.0).astype(BF16))
        y_diag = jnp.dot(jnp.concatenate(ms, axis=1), jnp.concatenate(xblocks, axis=0),
                         preferred_element_type=F32)
        hg = state_ref[g]
        y_off = jnp.dot(cg, hg.astype(BF16), preferred_element_type=F32) * eac_e[:, lanes]
        new_states = jnp.dot(bg.T.astype(BF16), xd[:, lanes], preferred_element_type=F32)
        state_ref[g] = hg * cdec_e[:, lanes] + new_states
        ybuf[:, lanes] = y_diag + y_off + dskip_ref[:, lanes] * xs[:, lanes]

    z = z_ref[...].astype(F32)
    yz = ybuf[...] * _silu(z)
    ya = yz * lax.rsqrt(jnp.mean(yz * yz, axis=-1, keepdims=True) + EPS) * ng_ref[...]
    o_ref[...] = ya.astype(BF16)


def _ssd(proj, dt_raw, conv_w, conv_b, dt_bias, a_log, d_skip, norm_g, seq):
    t = proj.shape[0]
    L, H, P, G, N = SSD_CHUNK, SSD_HEADS, SSD_HEAD_DIM, SSD_GROUPS, SSD_STATE
    xi = H * P
    bcw = 2 * G * N
    nc = seq // L
    pad = LANES - H
    expand = np.zeros((LANES, xi), np.float32)
    expand[np.arange(xi) // P, np.arange(xi)] = 1.0
    expand2 = jnp.asarray(np.concatenate([expand, expand], axis=0), BF16)
    row = lambda v: jnp.pad(v.astype(F32), (0, pad)).reshape(1, LANES)
    full = lambda shape: pl.BlockSpec(shape, lambda b, c: (0, 0))
    return pl.pallas_call(
        _ssd_kernel,
        grid=(t // seq, nc),
        in_specs=[pl.BlockSpec((L, xi), lambda b, c: (b * nc + c, 0)),
                  pl.BlockSpec((L, xi), lambda b, c: (b * nc + c, 1)),
                  pl.BlockSpec((L, bcw), lambda b, c: (b * nc + c, 2)),
                  pl.BlockSpec((L, LANES), lambda b, c: (b * nc + c, 0)),
                  full((SSD_CONV, xi + bcw)), full((1, xi + bcw)),
                  full((1, LANES)), full((1, LANES)), full((1, xi)), full((1, xi)),
                  full((2 * LANES, xi))],
        out_specs=pl.BlockSpec((L, xi), lambda b, c: (b * nc + c, 0)),
        out_shape=jax.ShapeDtypeStruct((t, xi), BF16),
        scratch_shapes=[pltpu.VMEM((SUBLANES + L, xi), F32),
                        pltpu.VMEM((SUBLANES + L, bcw), F32),
                        pltpu.VMEM((L, xi), F32),
                        pltpu.VMEM((G, N, (H // G) * P), F32)],
        compiler_params=_params("arbitrary", "arbitrary"),
        name="ssd",
    )(proj, proj, proj, dt_raw, conv_w, conv_b.reshape(1, -1), row(dt_bias), row(a_log),
      jnp.repeat(d_skip.astype(F32), P).reshape(1, xi), norm_g.reshape(1, xi), expand2)


def _gmlp_kernel(u_ref, v_ref, lng_ref, lnb_ref, ws_ref, bs_ref, o_ref, *, nchunk):
    L, NG, GD = GMLP_CHUNK, GMLP_GROUPS, GMLP_GROUP_DIM
    row = lax.broadcasted_iota(jnp.int32, (L, L), 0)
    col = lax.broadcasted_iota(jnp.int32, (L, L), 1)
    tril = col <= row
    for ch in range(nchunk):
        rows = slice(ch * L, (ch + 1) * L)
        ug = _gelu(u_ref[rows, :].astype(F32))
        vg = _gelu(v_ref[rows, :].astype(F32))
        mu = jnp.mean(vg, axis=-1, keepdims=True)
        xc = vg - mu
        var = jnp.mean(xc * xc, axis=-1, keepdims=True)
        vn = (xc * lax.rsqrt(var + EPS) * lng_ref[...] + lnb_ref[...]).astype(BF16)
        for g in range(NG):
            cols = slice(g * GD, (g + 1) * GD)
            w = jnp.where(tril, ws_ref[g], 0.0).astype(BF16)
            vv = jnp.dot(w, vn[:, cols], preferred_element_type=F32) + bs_ref[:, cols]
            o_ref[rows, cols] = (ug[:, cols] * vv).astype(BF16)


def _gmlp(proj, ln_g, ln_b, ws, bs, nchunk=2):
    t = proj.shape[0]
    L, NG, GD = GMLP_CHUNK, GMLP_GROUPS, GMLP_GROUP_DIM
    w = NG * GD
    tm = nchunk * L
    bs_full = jnp.repeat(bs.T.astype(F32), GD, axis=1)
    return pl.pallas_call(
        functools.partial(_gmlp_kernel, nchunk=nchunk),
        grid=(t // tm,),
        in_specs=[pl.BlockSpec((tm, w), lambda i: (i, 3)),
                  pl.BlockSpec((tm, w), lambda i: (i, 4)),
                  pl.BlockSpec((1, w), lambda i: (0, 0)),
                  pl.BlockSpec((1, w), lambda i: (0, 0)),
                  pl.BlockSpec((NG, L, L), lambda i: (0, 0, 0)),
                  pl.BlockSpec((L, w), lambda i: (0, 0))],
        out_specs=pl.BlockSpec((tm, w), lambda i: (i, 0)),
        out_shape=jax.ShapeDtypeStruct((t, w), BF16),
        compiler_params=_params("arbitrary"),
        name="gmlp",
    )(proj, proj, ln_g.reshape(1, w), ln_b.reshape(1, w), ws, bs_full)


def _attn_kernel(q_ref, k_ref, v_ref, o_ref, *, tq, strip):
    i = pl.program_id(2)
    tk = tq // 2
    n_strips = tq // strip
    r2 = lax.broadcasted_iota(jnp.int32, (tk, tk), 0)
    c2 = lax.broadcasted_iota(jnp.int32, (tk, tk), 1)
    after = jnp.where(r2 >= c2, 1.0, 0.0).astype(BF16)
    row = lax.broadcasted_iota(jnp.int32, (strip, tk), 0)
    col = lax.broadcasted_iota(jnp.int32, (strip, tk), 1)

    def tiles(js, carry, diag):
        kbs = [k_ref[pl.ds(pl.multiple_of(j * tk, tk), tk), :] for j in js]
        vbs = [v_ref[pl.ds(pl.multiple_of(j * tk, tk), tk), :] for j in js]
        chains, masks = [], {}
        for s in range(n_strips):
            for b in range(len(js)):
                shift = s * strip - (len(js) - 1 - b) * tk if diag else tk
                if shift + strip - 1 <= 0:
                    continue
                chains.append((s, b))
                masks[(s, b)] = None if shift >= tk else (col < row + shift)
        logits = {(s, b): lax.dot_general(q_ref[s * strip:(s + 1) * strip, :], kbs[b], (((1,), (1,)), ((), ())),
                                          preferred_element_type=F32) for s, b in chains}
        log_1m, split = {}, {}
        for c in chains:
            m = -(jnp.maximum(logits[c], 0.0) + jnp.log(1.0 + jnp.exp(-jnp.abs(logits[c]))))
            if masks[c] is not None:
                m = jnp.where(masks[c], m, 0.0)
            log_1m[c] = m
            split[c] = m.astype(BF16)
        tail = {c: jnp.dot(split[c], after, preferred_element_type=F32) for c in chains}
        runs = {}
        new_run = []
        for s in range(n_strips):
            run = carry[2 * s + 1]
            for b in range(len(js)):
                if (s, b) in masks:
                    runs[(s, b)] = run
                    run = run + jnp.sum(log_1m[(s, b)], axis=-1, keepdims=True)
            new_run.append(run)
        weights = {}
        for c in chains:
            w = jnp.exp(logits[c] + tail[c] + runs[c])
            if masks[c] is not None:
                w = jnp.where(masks[c], w, 0.0)
            weights[c] = w.astype(BF16)
        pv = {c: jnp.dot(weights[c], vbs[c[1]], preferred_element_type=F32) for c in chains}
        out = []
        for s in range(n_strips):
            acc = carry[2 * s]
            for b in range(len(js)):
                if (s, b) in pv:
                    acc = acc + pv[(s, b)]
            out.extend([acc, new_run[s]])
        return tuple(out)

    init = tuple(jnp.zeros((strip, SB_HEAD_DIM if k % 2 == 0 else 1), F32) for k in range(2 * n_strips))
    carry = tiles([2 * i + 1, 2 * i], init, True)
    carry = lax.fori_loop(0, i, lambda p, c: tiles([2 * (i - p) - 1, 2 * (i - p) - 2], c, False), carry)
    for s in range(n_strips):
        o_ref[s * strip:(s + 1) * strip, :] = carry[2 * s].astype(BF16)


def _attn(qkv, seq, tq=512, strip=128):
    t = qkv.shape[0]
    hd, nh = SB_HEAD_DIM, SB_HEADS
    tq = min(tq, seq)
    nq = seq // tq
    return pl.pallas_call(
        functools.partial(_attn_kernel, tq=tq, strip=min(strip, tq)),
        grid=(t // seq, nh, nq),
        in_specs=[pl.BlockSpec((tq, hd), lambda b, h, i: (b * nq + i, h)),
                  pl.BlockSpec((seq, hd), lambda b, h, i: (b, nh + h)),
                  pl.BlockSpec((seq, hd), lambda b, h, i: (b, 2 * nh + h))],
        out_specs=pl.BlockSpec((tq, hd), lambda b, h, i: (b * nq + i, h)),
        out_shape=jax.ShapeDtypeStruct((t, nh * hd), BF16),
        compiler_params=_params("arbitrary", "arbitrary", "arbitrary"),
        name="sb_attn",
    )(qkv, qkv, qkv)


RANK_SCALE = 2.0 ** 100
RANK_NONE = 64.0


def _extract_top(work, count):
    tops = []
    for k in range(count):
        m = jnp.max(work, axis=0, keepdims=True)
        tops.append(m)
        work = jnp.where(work == m, -(2.0 - k / 32.0) * RANK_SCALE, work)
    return tops, RANK_NONE + work * (32.0 / RANK_SCALE)


def _peer_topk_kernel(ht_ref, wqt_ref, keys_ref, l0_ref, p0_ref, r1_ref, p1z_ref, q_scr, cand_scr):
    nh, kd, topk = PEER_HEADS, PEER_HALF, PEER_TOPK
    n_cand = sum((topk + 1) // (k + 1) for k in range(topk + 1))
    q_scr[...] = jnp.dot(wqt_ref[...], ht_ref[...], preferred_element_type=F32).astype(BF16)
    cand_scr[...] = jnp.full(cand_scr.shape, NEG, F32)

    def head(h, carry):
        scores, tops, ranks = [], [], []
        for half in range(2):
            start = pl.multiple_of(h * (2 * kd) + half * kd, kd)
            sc = jnp.dot(keys_ref[h, half], q_scr[pl.ds(start, kd), :], preferred_element_type=F32)
            scores.append(sc)
            top, rank = _extract_top(sc, topk + 1)
            tops.append(top)
            ranks.append(rank)
        r = 0
        for k in range(topk + 1):
            for l in range((topk + 1) // (k + 1)):
                cand_scr[pl.ds(r, 1), :] = tops[0][k] + tops[1][l]
                r += 1
        cand = cand_scr[...]
        best, _ = _extract_top(cand, topk + 1)
        theta = 0.5 * (best[topk - 1] + best[topk])
        zsum = jnp.sum(jnp.where(cand > theta, jnp.exp(cand - best[0]), 0.0), axis=0, keepdims=True)
        margin = scores[0] - theta
        l0 = None
        for l in range(topk + 1):
            hit = jnp.clip((margin + tops[1][l]) * RANK_SCALE, 0.0, 1.0)
            l0 = hit if l0 is None else l0 + hit
        l0_ref[h] = l0
        p0_ref[h] = jnp.exp(scores[0] - tops[0][0])
        r1_ref[h] = ranks[1].astype(BF16)
        p1z_ref[h] = (jnp.exp(scores[1] - tops[1][0]) / zsum).astype(BF16)
        return carry

    assert n_cand <= cand_scr.shape[0]
    lax.fori_loop(0, nh, head, 0)


def _peer_topk(ht, wqt, keys, tb=256):
    d, t = ht.shape
    nh, nk, kd = PEER_HEADS, PEER_KEYS, PEER_HALF
    n_cand = sum((PEER_TOPK + 1) // (k + 1) for k in range(PEER_TOPK + 1))
    cand_rows = -(-n_cand // SUBLANES) * SUBLANES
    out_spec = pl.BlockSpec((nh, nk, tb), lambda i: (0, 0, i))
    out_shape = jax.ShapeDtypeStruct((nh, nk, t), F32)
    return pl.pallas_call(
        _peer_topk_kernel,
        grid=(t // tb,),
        in_specs=[pl.BlockSpec((d, tb), lambda i: (0, i)),
                  pl.BlockSpec((nh * 2 * kd, d), lambda i: (0, 0)),
                  pl.BlockSpec((nh, 2, nk, kd), lambda i: (0, 0, 0, 0))],
        out_specs=[out_spec] * 4,
        out_shape=[out_shape, out_shape, jax.ShapeDtypeStruct((nh, nk, t), BF16),
                   jax.ShapeDtypeStruct((nh, nk, t), BF16)],
        scratch_shapes=[pltpu.VMEM((nh * 2 * kd, tb), BF16),
                        pltpu.VMEM((cand_rows, tb), F32)],
        compiler_params=_params("arbitrary"),
        name="peer_topk",
    )(ht, wqt, keys)


def _peer_dense_kernel(ht_ref, u_ref, vt_ref, l0_ref, p0_ref, r1_ref, p1z_ref, x_ref, gate_ref, *rest,
                       ni, tb, n_blocks, last_layer):
    n_post = 1 if last_layer else 3
    post_refs = rest[:n_post]
    o_ref = rest[n_post]
    h_ref = None if last_layer else rest[n_post + 1]
    acc_ref, g0_scr, g1_scr, a_scr = rest[-4:]
    nh, nk = PEER_HEADS, PEER_KEYS
    d = acc_ref.shape[0]
    e = pl.program_id(1)
    pair = 2 * nk
    n_chunks = ni * nk // pair
    acc_rows = d // n_chunks

    @pl.when(e == 0)
    def _():
        acc_ref[...] = jnp.zeros(acc_ref.shape, F32)
        g1_scr[...] = jnp.zeros(g1_scr.shape, BF16)

    def gate_chunk(c, g_cur):
        for sub in range(pair // nk):
            ii = c * (pair // nk) + sub
            for cc in range(tb // LANES):
                cs = slice(cc * LANES, (cc + 1) * LANES)
                for j0 in range(0, nk, GATE_ROWS):
                    js = slice(j0, j0 + GATE_ROWS)
                    w = None
                    for h in range(nh):
                        l0 = l0_ref[ii, h:h + 1, cs].astype(BF16)
                        p0 = p0_ref[ii, h:h + 1, cs].astype(BF16)
                        hit = jnp.clip(l0 - r1_ref[h, js, cs], 0.0, 1.0)
                        term = hit * p1z_ref[h, js, cs] * p0
                        w = term if w is None else w + term
                    a = a_scr[sub * nk + j0:sub * nk + j0 + GATE_ROWS, cs]
                    g_cur[ii * nk + j0:ii * nk + j0 + GATE_ROWS, cs] = _gelu(a).astype(BF16) * w

    def first_product(c):
        a_scr[...] = jnp.dot(u_ref[c * pair:(c + 1) * pair, :], ht_ref[...], preferred_element_type=F32)

    def step(g_cur, g_prev):
        for c in range(n_chunks):
            first_product(c)
            rows = slice(c * acc_rows, (c + 1) * acc_rows)
            acc_ref[rows, :] += jnp.dot(vt_ref[rows, :], g_prev[...], preferred_element_type=F32)
            gate_chunk(c, g_cur)

    @pl.when(jnp.logical_and(e < n_blocks, e % 2 == 0))
    def _():
        step(g0_scr, g1_scr)

    @pl.when(jnp.logical_and(e < n_blocks, e % 2 == 1))
    def _():
        step(g1_scr, g0_scr)

    @pl.when(e == n_blocks)
    def _():
        g_last = g1_scr if n_blocks % 2 == 0 else g0_scr
        total = acc_ref[...] + jnp.dot(vt_ref[...], g_last[...], preferred_element_type=F32)
        x_new = x_ref[...] + gate_ref[0] * total.T
        normed = x_new * lax.rsqrt(jnp.mean(x_new * x_new, axis=-1, keepdims=True) + EPS) * post_refs[0][...]
        if last_layer:
            o_ref[...] = normed
        else:
            o_ref[...] = x_new
            h_ref[...] = (normed * (1.0 + post_refs[2][0]) + post_refs[1][0]).astype(BF16)


def _peer_dense(ht, u, vt, l0, p0, r1, p1z, x2, gate, post, seq, tb=512, eb=512):
    d, t = ht.shape
    last_layer = len(post) == 1
    n_exp = u.shape[0]
    nh, nk = PEER_HEADS, PEER_KEYS
    tb, eb = min(tb, seq), min(eb, n_exp)
    ni = eb // nk
    n_blocks = n_exp // eb
    per_batch = seq // tb
    blk = lambda b: jnp.clip(b, 0, n_blocks - 1)
    score_spec = pl.BlockSpec((nh, nk, tb), lambda i, e: (0, 0, i))
    first_spec = pl.BlockSpec((ni, nh, tb), lambda i, e: (blk(e), 0, i))
    batch_spec = pl.BlockSpec((1, 1, d), lambda i, e: (i // per_batch, 0, 0))
    row_spec = pl.BlockSpec((tb, d), lambda i, e: (i, 0))
    post_specs = [pl.BlockSpec((1, d), lambda i, e: (0, 0))] + [batch_spec] * (len(post) - 1)
    post_args = [post[0].reshape(1, d)] + list(post[1:])
    out_specs = [row_spec] if last_layer else [row_spec, row_spec]
    out_shape = [jax.ShapeDtypeStruct((t, d), F32)] + ([] if last_layer else [jax.ShapeDtypeStruct((t, d), BF16)])
    return pl.pallas_call(
        functools.partial(_peer_dense_kernel, ni=ni, tb=tb, n_blocks=n_blocks, last_layer=last_layer),
        grid=(t // tb, n_blocks + 1),
        in_specs=[pl.BlockSpec((d, tb), lambda i, e: (0, i)),
                  pl.BlockSpec((eb, d), lambda i, e: (blk(e), 0)),
                  pl.BlockSpec((d, eb), lambda i, e: (0, blk(e - 1))),
                  first_spec, first_spec, score_spec, score_spec,
                  pl.BlockSpec((tb, d), lambda i, e: (i, 0), pipeline_mode=pl.Buffered(1)),
                  batch_spec] + post_specs,
        out_specs=out_specs,
        out_shape=out_shape,
        scratch_shapes=[pltpu.VMEM((d, tb), F32),
                        pltpu.VMEM((eb, tb), BF16),
                        pltpu.VMEM((eb, tb), BF16),
                        pltpu.VMEM((2 * nk, tb), F32)],
        compiler_params=_params("arbitrary", "arbitrary"),
        name="peer_dense",
    )(ht, u, vt, l0.transpose(1, 0, 2), p0.transpose(1, 0, 2), r1, p1z, x2, gate, *post_args)


def _cast_t_kernel(w_ref, o_ref):
    o_ref[...] = w_ref[...].T.astype(BF16)


def _cast_t(w, tm=512):
    m, n = w.shape
    return pl.pallas_call(
        _cast_t_kernel,
        grid=(m // tm,),
        in_specs=[pl.BlockSpec((tm, n), lambda i: (i, 0))],
        out_specs=pl.BlockSpec((n, tm), lambda i: (0, i)),
        out_shape=jax.ShapeDtypeStruct((n, m), BF16),
        compiler_params=_params("arbitrary"),
        name="cast_t",
    )(w)


def _peer(x2, g, shift, scale, gate, wq, keys, u, v, post, seq):
    ht = _normmod(x2, g, shift, scale, seq, transposed=True)
    l0, p0, r1, p1z = _peer_topk(ht, _cast_t(wq), keys.astype(BF16))
    return _peer_dense(ht, u.astype(BF16), _cast_t(v), l0, p0, r1, p1z, x2, gate, post, seq)


def kernel(x, c, ada_w, ada_b, norm_mix_g, norm_ffn_g, in0_w, conv_w, conv_b, dt_bias, a_log, d_skip,
           ssd_norm_g, gmlp_ln_g, gmlp_ln_b, gmlp_ws, gmlp_bs, out0_w, sb_qkv_w, sb_out_w, peer_wq,
           peer_keys, peer_u, peer_v, final_g):
    bsz, seq, d = x.shape
    depth = ada_w.shape[0]
    xi = SSD_HEADS * SSD_HEAD_DIM
    conv_dim = xi + 2 * SSD_GROUPS * SSD_STATE
    mod = _ada_mod(c, ada_w, ada_b)
    x2 = x.reshape(bsz * seq, d)
    mods = [[m.reshape(bsz, 1, d) for m in jnp.split(mod[i], 6, axis=-1)] for i in range(depth)]
    h = _normmod(x2, norm_mix_g[0], mods[0][0], mods[0][1], seq, transposed=False)
    for i in range(depth):
        shift1, scale1, gate1, shift2, scale2, gate2 = mods[i]
        j = i // 2
        if i % 2 == 0:
            w = in0_w[j]
            dt_cols = slice(xi + conv_dim, xi + conv_dim + SSD_HEADS)
            w_main = jnp.concatenate([w[:, :xi + conv_dim], w[:, dt_cols.stop:]], axis=1).astype(BF16)
            w_dt = jnp.pad(w[:, dt_cols], ((0, 0), (0, LANES - SSD_HEADS))).astype(BF16)
            proj = _matmul([h], [w_main], BF16, 1024, 512, name="in0_proj")
            dt_raw = _matmul([h], [w_dt], F32, 1024, LANES, name="in0_dt")
            ya = _ssd(proj, dt_raw, conv_w[j], conv_b[j], dt_bias[j], a_log[j], d_skip[j], ssd_norm_g[j], seq)
            yb = _gmlp(proj, gmlp_ln_g[j], gmlp_ln_b[j], gmlp_ws[j], gmlp_bs[j])
            wo = out0_w[j].astype(BF16)
            x2 = _matmul([ya, yb], [wo[:xi], wo[xi:]], F32, 1024, 512, resid=x2, gate=gate1, seq=seq,
                         name="out0_proj")
        else:
            w_qkv = sb_qkv_w[j]
            n_q = SB_HEADS * SB_HEAD_DIM
            w_qkv = jnp.concatenate([w_qkv[:, :n_q] * SB_HEAD_DIM ** -0.5, w_qkv[:, n_q:]], axis=1).astype(BF16)
            qkv = _matmul([h], [w_qkv], BF16, 1024, 512, name="qkv_proj")
            o = _attn(qkv, seq)
            x2 = _matmul([o], [sb_out_w[j].astype(BF16)], F32, 1024, 512, resid=x2, gate=gate1, seq=seq,
                         name="sb_out_proj")
        last = i + 1 == depth
        post = (final_g,) if last else (norm_mix_g[i + 1], mods[i + 1][0], mods[i + 1][1])
        res = _peer(x2, norm_ffn_g[i], shift2, scale2, gate2, peer_wq[i], peer_keys[i], peer_u[i],
                    peer_v[i], post, seq)
        if last:
            return res[0].reshape(bsz, seq, d)
        x2, h = res
```

```python
import functools

import numpy as np
import jax
import jax.numpy as jnp
from jax import lax
from jax.experimental import pallas as pl
from jax.experimental.pallas import tpu as pltpu

F32 = jnp.float32
BF16 = jnp.bfloat16
EPS = 1e-6
NEG = -1e30
INV_SQRT2 = 0.7071067811865476

SSD_HEADS = 32
SSD_HEAD_DIM = 64
SSD_GROUPS = 8
SSD_STATE = 128
SSD_CONV = 4
SSD_CHUNK = 128
GMLP_GROUPS = 16
GMLP_GROUP_DIM = 128
GMLP_CHUNK = 128
SB_HEADS = 16
SB_HEAD_DIM = 128
PEER_HEADS = 8
PEER_KEYS = 128
PEER_HALF = 128
PEER_TOPK = 16
GATE_ROWS = 128

LANES = 128
SUBLANES = 8
VMEM_LIMIT = 56 * 1024 * 1024


def _params(*sem, vmem=VMEM_LIMIT):
    return pltpu.CompilerParams(dimension_semantics=sem, vmem_limit_bytes=vmem)


def _gelu(x):
    return 0.5 * x * (1.0 + lax.erf(x * INV_SQRT2))


def _silu(x):
    return x * jax.nn.sigmoid(x)


def _softplus(x):
    return jnp.maximum(x, 0.0) + jnp.log1p(jnp.exp(-jnp.abs(x)))


def _split2(v):
    hi = v.astype(BF16)
    lo = (v - hi.astype(F32)).astype(BF16)
    return hi, lo


def _split3(v):
    hi = v.astype(BF16)
    r = v - hi.astype(F32)
    mid = r.astype(BF16)
    lo = (r - mid.astype(F32)).astype(BF16)
    return hi, mid, lo


def _ada_kernel(c_ref, w_ref, b_ref, o_ref):
    cond = _silu(c_ref[...]).astype(BF16)
    o_ref[0] = jnp.dot(cond, w_ref[0].astype(BF16), preferred_element_type=F32) + b_ref[0]


def _ada_mod(c, ada_w, ada_b, tn=512):
    depth, d, n = ada_w.shape
    bsz = c.shape[0]
    rows = -(-bsz // SUBLANES) * SUBLANES
    c_pad = jnp.pad(c, ((0, rows - bsz), (0, 0)))
    out = pl.pallas_call(
        _ada_kernel,
        grid=(depth, n // tn),
        in_specs=[pl.BlockSpec((rows, d), lambda i, j: (0, 0)),
                  pl.BlockSpec((1, d, tn), lambda i, j: (i, 0, j)),
                  pl.BlockSpec((1, 1, tn), lambda i, j: (i, 0, j))],
        out_specs=pl.BlockSpec((1, rows, tn), lambda i, j: (i, 0, j)),
        out_shape=jax.ShapeDtypeStruct((depth, rows, n), F32),
        compiler_params=_params("arbitrary", "arbitrary"),
        name="ada_mod",
    )(c_pad, ada_w, ada_b.reshape(depth, 1, n))
    return out[:, :bsz]


def _normmod_kernel(x_ref, g_ref, sh_ref, sc_ref, o_ref, *, transposed):
    x = x_ref[...]
    y = x * lax.rsqrt(jnp.mean(x * x, axis=-1, keepdims=True) + EPS) * g_ref[...]
    h = y * (1.0 + sc_ref[0]) + sh_ref[0]
    o_ref[...] = (h.T if transposed else h).astype(BF16)


def _normmod(x2, g, shift, scale, seq, transposed, tm=512):
    t, d = x2.shape
    tm = min(tm, seq)
    per_batch = seq // tm
    if transposed:
        out_spec = pl.BlockSpec((d, tm), lambda i: (0, i))
        out_shape = jax.ShapeDtypeStruct((d, t), BF16)
    else:
        out_spec = pl.BlockSpec((tm, d), lambda i: (i, 0))
        out_shape = jax.ShapeDtypeStruct((t, d), BF16)
    return pl.pallas_call(
        functools.partial(_normmod_kernel, transposed=transposed),
        grid=(t // tm,),
        in_specs=[pl.BlockSpec((tm, d), lambda i: (i, 0)),
                  pl.BlockSpec((1, d), lambda i: (0, 0)),
                  pl.BlockSpec((1, 1, d), lambda i: (i // per_batch, 0, 0)),
                  pl.BlockSpec((1, 1, d), lambda i: (i // per_batch, 0, 0))],
        out_specs=out_spec,
        out_shape=out_shape,
        compiler_params=_params("arbitrary"),
        name="normmod_t" if transposed else "normmod",
    )(x2, g.reshape(1, d), shift, scale)


def _mm_kernel(*refs, n_pairs, has_resid):
    acc = None
    for a_ref, b_ref in zip(refs[:n_pairs], refs[n_pairs:2 * n_pairs]):
        part = jnp.dot(a_ref[...], b_ref[...], preferred_element_type=F32)
        acc = part if acc is None else acc + part
    if has_resid:
        x_ref, gate_ref, o_ref = refs[2 * n_pairs:]
        o_ref[...] = x_ref[...] + gate_ref[0] * acc
    else:
        o_ref = refs[2 * n_pairs]
        o_ref[...] = acc.astype(o_ref.dtype)


def _matmul(a_list, b_list, out_dtype, tm, tn, resid=None, gate=None, seq=None, name="matmul"):
    m = a_list[0].shape[0]
    n = b_list[0].shape[1]
    tm, tn = min(tm, m if seq is None else seq), min(tn, n)
    in_specs = [pl.BlockSpec((tm, a.shape[1]), lambda i, j: (i, 0)) for a in a_list]
    in_specs += [pl.BlockSpec((b.shape[0], tn), lambda i, j: (0, j)) for b in b_list]
    args = list(a_list) + list(b_list)
    if resid is not None:
        per_batch = seq // tm
        in_specs += [pl.BlockSpec((tm, tn), lambda i, j: (i, j)),
                     pl.BlockSpec((1, 1, tn), lambda i, j: (i // per_batch, 0, j))]
        args += [resid, gate]
    return pl.pallas_call(
        functools.partial(_mm_kernel, n_pairs=len(a_list), has_resid=resid is not None),
        grid=(m // tm, n // tn),
        in_specs=in_specs,
        out_specs=pl.BlockSpec((tm, tn), lambda i, j: (i, j)),
        out_shape=jax.ShapeDtypeStruct((m, n), out_dtype),
        compiler_params=_params("arbitrary", "arbitrary"),
        name=name,
    )(*args)


def _ssd_kernel(z_ref, xs_ref, bc_ref, dt_ref, cw_ref, cb_ref, dtb_ref, alog_ref, dskip_ref, ng_ref,
                expand_ref, o_ref, xbuf, bcbuf, ybuf, state_ref):
    L, H, P, G, N = SSD_CHUNK, SSD_HEADS, SSD_HEAD_DIM, SSD_GROUPS, SSD_STATE
    R = H // G
    XI = H * P
    HALO = SUBLANES

    @pl.when(pl.program_id(1) == 0)
    def _():
        xbuf[0:HALO, :] = jnp.zeros((HALO, XI), F32)
        bcbuf[0:HALO, :] = jnp.zeros((HALO, 2 * G * N), F32)
        state_ref[...] = jnp.zeros(state_ref.shape, F32)

    xbuf[HALO:HALO + L, :] = xs_ref[...].astype(F32)
    bcbuf[HALO:HALO + L, :] = bc_ref[...].astype(F32)

    def conv_silu(buf, w, b):
        acc = b
        for k in range(SSD_CONV):
            off = HALO - (SSD_CONV - 1) + k
            acc = acc + w[k:k + 1, :] * buf[off:off + L, :]
        return _silu(acc)

    cw = cw_ref[...]
    cb = cb_ref[...]
    xs = conv_silu(xbuf, cw[:, :XI], cb[:, :XI])
    bc = conv_silu(bcbuf, cw[:, XI:], cb[:, XI:])
    xbuf[0:HALO, :] = xbuf[L:L + HALO, :]
    bcbuf[0:HALO, :] = bcbuf[L:L + HALO, :]

    dt = _softplus(dt_ref[...] + dtb_ref[...])
    a = dt * (-jnp.exp(alog_ref[...]))
    row = lax.broadcasted_iota(jnp.int32, (L, L), 0)
    col = lax.broadcasted_iota(jnp.int32, (L, L), 1)
    causal = col <= row
    tri = jnp.where(causal, 1.0, 0.0).astype(BF16)
    a_cum = sum(jnp.dot(tri, part, preferred_element_type=F32) for part in _split3(a))
    a_cum_t = a_cum.T
    a_last = a_cum[L - 1:L, :]
    dte = jnp.exp(a_last - a_cum)
    eac = jnp.exp(a_cum)

    stk = jnp.concatenate([dt, dt * dte, eac], axis=0)
    hi, lo = _split2(stk)
    ex = jnp.dot(jnp.concatenate([hi, lo], axis=1), expand_ref[...], preferred_element_type=F32)
    dt_e, dd_e, eac_e = ex[0:L], ex[L:2 * L], ex[2 * L:3 * L]
    x_in = xs * dt_e
    xd = (xs * dd_e).astype(BF16)
    cdec_e = eac_e[L - 1:L, :]

    GW = R * P
    lane_head = lax.broadcasted_iota(jnp.int32, (L, GW), 1) // P
    for g in range(G):
        lanes = slice(g * GW, (g + 1) * GW)
        bg = bc[:, g * N:(g + 1) * N]
        cg = bc[:, G * N + g * N:G * N + (g + 1) * N].astype(BF16)
        cbm = lax.dot_general(cg, bg.astype(BF16), (((1,), (1,)), ((), ())), preferred_element_type=F32)
        xg = x_in[:, lanes]
        ms, xblocks = [], []
        for r in range(R):
            h = g * R + r
            seg = a_cum[:, h:h + 1] - a_cum_t[h:h + 1, :]
            dec = jnp.where(causal, jnp.exp(jnp.where(causal, seg, 0.0)), 0.0)
            ms.append((cbm * dec).astype(BF16))
            xblocks.append(jnp.where(lane_head == r, xg, 0.0).astype(BF16))
        y_diag = jnp.dot(jnp.concatenate(ms, axis=1), jnp.concatenate(xblocks, axis=0),
                         preferred_element_type=F32)
        hg = state_ref[g]
        y_off = jnp.dot(cg, hg.astype(BF16), preferred_element_type=F32) * eac_e[:, lanes]
        new_states = jnp.dot(bg.T.astype(BF16), xd[:, lanes], preferred_element_type=F32)
        state_ref[g] = hg * cdec_e[:, lanes] + new_states
        ybuf[:, lanes] = y_diag + y_off + dskip_ref[:, lanes] * xs[:, lanes]

    z = z_ref[...].astype(F32)
    yz = ybuf[...] * _silu(z)
    ya = yz * lax.rsqrt(jnp.mean(yz * yz, axis=-1, keepdims=True) + EPS) * ng_ref[...]
    o_ref[...] = ya.astype(BF16)


def _ssd(proj, dt_raw, conv_w, conv_b, dt_bias, a_log, d_skip, norm_g, seq):
    t = proj.shape[0]
    L, H, P, G, N = SSD_CHUNK, SSD_HEADS, SSD_HEAD_DIM, SSD_GROUPS, SSD_STATE
    xi = H * P
    bcw = 2 * G * N
    nc = seq // L
    pad = LANES - H
    expand = np.zeros((LANES, xi), np.float32)
    expand[np.arange(xi) // P, np.arange(xi)] = 1.0
    expand2 = jnp.asarray(np.concatenate([expand, expand], axis=0), BF16)
    row = lambda v: jnp.pad(v.astype(F32), (0, pad)).reshape(1, LANES)
    full = lambda shape: pl.BlockSpec(shape, lambda b, c: (0, 0))
    return pl.pallas_call(
        _ssd_kernel,
        grid=(t // seq, nc),
        in_specs=[pl.BlockSpec((L, xi), lambda b, c: (b * nc + c, 0)),
                  pl.BlockSpec((L, xi), lambda b, c: (b * nc + c, 1)),
                  pl.BlockSpec((L, bcw), lambda b, c: (b * nc + c, 2)),
                  pl.BlockSpec((L, LANES), lambda b, c: (b * nc + c, 0)),
                  full((SSD_CONV, xi + bcw)), full((1, xi + bcw)),
                  full((1, LANES)), full((1, LANES)), full((1, xi)), full((1, xi)),
                  full((2 * LANES, xi))],
        out_specs=pl.BlockSpec((L, xi), lambda b, c: (b * nc + c, 0)),
        out_shape=jax.ShapeDtypeStruct((t, xi), BF16),
        scratch_shapes=[pltpu.VMEM((SUBLANES + L, xi), F32),
                        pltpu.VMEM((SUBLANES + L, bcw), F32),
                        pltpu.VMEM((L, xi), F32),
                        pltpu.VMEM((G, N, (H // G) * P), F32)],
        compiler_params=_params("arbitrary", "arbitrary"),
        name="ssd",
    )(proj, proj, proj, dt_raw, conv_w, conv_b.reshape(1, -1), row(dt_bias), row(a_log),
      jnp.repeat(d_skip.astype(F32), P).reshape(1, xi), norm_g.reshape(1, xi), expand2)


def _gmlp_kernel(u_ref, v_ref, lng_ref, lnb_ref, ws_ref, bs_ref, o_ref, *, nchunk):
    L, NG, GD = GMLP_CHUNK, GMLP_GROUPS, GMLP_GROUP_DIM
    row = lax.broadcasted_iota(jnp.int32, (L, L), 0)
    col = lax.broadcasted_iota(jnp.int32, (L, L), 1)
    tril = col <= row
    for ch in range(nchunk):
        rows = slice(ch * L, (ch + 1) * L)
        ug = _gelu(u_ref[rows, :].astype(F32))
        vg = _gelu(v_ref[rows, :].astype(F32))
        mu = jnp.mean(vg, axis=-1, keepdims=True)
        xc = vg - mu
        var = jnp.mean(xc * xc, axis=-1, keepdims=True)
        vn = (xc * lax.rsqrt(var + EPS) * lng_ref[...] + lnb_ref[...]).astype(BF16)
        for g in range(NG):
            cols = slice(g * GD, (g + 1) * GD)
            w = jnp.where(tril, ws_ref[g], 0.0).astype(BF16)
            vv = jnp.dot(w, vn[:, cols], preferred_element_type=F32) + bs_ref[:, cols]
            o_ref[rows, cols] = (ug[:, cols] * vv).astype(BF16)


def _gmlp(proj, ln_g, ln_b, ws, bs, nchunk=2):
    t = proj.shape[0]
    L, NG, GD = GMLP_CHUNK, GMLP_GROUPS, GMLP_GROUP_DIM
    w = NG * GD
    tm = nchunk * L
    bs_full = jnp.repeat(bs.T.astype(F32), GD, axis=1)
    return pl.pallas_call(
        functools.partial(_gmlp_kernel, nchunk=nchunk),
        grid=(t // tm,),
        in_specs=[pl.BlockSpec((tm, w), lambda i: (i, 3)),
                  pl.BlockSpec((tm, w), lambda i: (i, 4)),
                  pl.BlockSpec((1, w), lambda i: (0, 0)),
                  pl.BlockSpec((1, w), lambda i: (0, 0)),
                  pl.BlockSpec((NG, L, L), lambda i: (0, 0, 0)),
                  pl.BlockSpec((L, w), lambda i: (0, 0))],
        out_specs=pl.BlockSpec((tm, w), lambda i: (i, 0)),
        out_shape=jax.ShapeDtypeStruct((t, w), BF16),
        compiler_params=_params("arbitrary"),
        name="gmlp",
    )(proj, proj, ln_g.reshape(1, w), ln_b.reshape(1, w), ws, bs_full)


def _attn_kernel(q_ref, k_ref, v_ref, o_ref, *, tq, strip):
    i = pl.program_id(2)
    tk = tq // 2
    n_strips = tq // strip
    r2 = lax.broadcasted_iota(jnp.int32, (tk, tk), 0)
    c2 = lax.broadcasted_iota(jnp.int32, (tk, tk), 1)
    after = jnp.where(r2 >= c2, 1.0, 0.0).astype(BF16)
    row = lax.broadcasted_iota(jnp.int32, (strip, tk), 0)
    col = lax.broadcasted_iota(jnp.int32, (strip, tk), 1)

    def tiles(js, carry, diag):
        kbs = [k_ref[pl.ds(pl.multiple_of(j * tk, tk), tk), :] for j in js]
        vbs = [v_ref[pl.ds(pl.multiple_of(j * tk, tk), tk), :] for j in js]
        chains, masks = [], {}
        for s in range(n_strips):
            for b in range(len(js)):
                shift = s * strip - (len(js) - 1 - b) * tk if diag else tk
                if shift + strip - 1 <= 0:
                    continue
                chains.append((s, b))
                masks[(s, b)] = None if shift >= tk else (col < row + shift)
        logits = {(s, b): lax.dot_general(q_ref[s * strip:(s + 1) * strip, :], kbs[b], (((1,), (1,)), ((), ())),
                                          preferred_element_type=F32) for s, b in chains}
        log_1m, split = {}, {}
        for c in chains:
            m = -(jnp.maximum(logits[c], 0.0) + jnp.log(1.0 + jnp.exp(-jnp.abs(logits[c]))))
            if masks[c] is not None:
                m = jnp.where(masks[c], m, 0.0)
            log_1m[c] = m
            split[c] = m.astype(BF16)
        tail = {c: jnp.dot(split[c], after, preferred_element_type=F32) for c in chains}
        runs = {}
        new_run = []
        for s in range(n_strips):
            run = carry[2 * s + 1]
            for b in range(len(js)):
                if (s, b) in masks:
                    runs[(s, b)] = run
                    run = run + jnp.sum(log_1m[(s, b)], axis=-1, keepdims=True)
            new_run.append(run)
        weights = {}
        for c in chains:
            w = jnp.exp(logits[c] + tail[c] + runs[c])
            if masks[c] is not None:
                w = jnp.where(masks[c], w, 0.0)
            weights[c] = w.astype(BF16)
        pv = {c: jnp.dot(weights[c], vbs[c[1]], preferred_element_type=F32) for c in chains}
        out = []
        for s in range(n_strips):
            acc = carry[2 * s]
            for b in range(len(js)):
                if (s, b) in pv:
                    acc = acc + pv[(s, b)]
            out.extend([acc, new_run[s]])
        return tuple(out)

    init = tuple(jnp.zeros((strip, SB_HEAD_DIM if k % 2 == 0 else 1), F32) for k in range(2 * n_strips))
    carry = tiles([2 * i + 1, 2 * i], init, True)
    carry = lax.fori_loop(0, i, lambda p, c: tiles([2 * (i - p) - 1, 2 * (i - p) - 2], c, False), carry)
    for s in range(n_strips):
        o_ref[s * strip:(s + 1) * strip, :] = carry[2 * s].astype(BF16)


def _attn(qkv, seq, tq=512, strip=128):
    t = qkv.shape[0]
    hd, nh = SB_HEAD_DIM, SB_HEADS
    tq = min(tq, seq)
    nq = seq // tq
    return pl.pallas_call(
        functools.partial(_attn_kernel, tq=tq, strip=min(strip, tq)),
        grid=(t // seq, nh, nq),
        in_specs=[pl.BlockSpec((tq, hd), lambda b, h, i: (b * nq + i, h)),
                  pl.BlockSpec((seq, hd), lambda b, h, i: (b, nh + h)),
                  pl.BlockSpec((seq, hd), lambda b, h, i: (b, 2 * nh + h))],
        out_specs=pl.BlockSpec((tq, hd), lambda b, h, i: (b * nq + i, h)),
        out_shape=jax.ShapeDtypeStruct((t, nh * hd), BF16),
        compiler_params=_params("arbitrary", "arbitrary", "arbitrary"),
        name="sb_attn",
    )(qkv, qkv, qkv)


RANK_SCALE = 2.0 ** 100
RANK_NONE = 64.0


def _extract_top(work, count):
    tops = []
    for k in range(count):
        m = jnp.max(work, axis=0, keepdims=True)
        tops.append(m)
        work = jnp.where(work == m, -(2.0 - k / 32.0) * RANK_SCALE, work)
    return tops, RANK_NONE + work * (32.0 / RANK_SCALE)


def _peer_topk_kernel(ht_ref, wqt_ref, keys_ref, l0_ref, p0_ref, r1_ref, p1z_ref, q_scr, cand_scr):
    nh, kd, topk = PEER_HEADS, PEER_HALF, PEER_TOPK
    n_cand = sum((topk + 1) // (k + 1) for k in range(topk + 1))
    q_scr[...] = jnp.dot(wqt_ref[...], ht_ref[...], preferred_element_type=F32).astype(BF16)
    cand_scr[...] = jnp.full(cand_scr.shape, NEG, F32)

    def head(h, carry):
        scores, tops, ranks = [], [], []
        for half in range(2):
            start = pl.multiple_of(h * (2 * kd) + half * kd, kd)
            sc = jnp.dot(keys_ref[h, half], q_scr[pl.ds(start, kd), :], preferred_element_type=F32)
            scores.append(sc)
            top, rank = _extract_top(sc, topk + 1)
            tops.append(top)
            ranks.append(rank)
        r = 0
        for k in range(topk + 1):
            for l in range((topk + 1) // (k + 1)):
                cand_scr[pl.ds(r, 1), :] = tops[0][k] + tops[1][l]
                r += 1
        cand = cand_scr[...]
        best, _ = _extract_top(cand, topk + 1)
        theta = 0.5 * (best[topk - 1] + best[topk])
        zsum = jnp.sum(jnp.where(cand > theta, jnp.exp(cand - best[0]), 0.0), axis=0, keepdims=True)
        margin = scores[0] - theta
        l0 = None
        for l in range(topk + 1):
            hit = jnp.clip((margin + tops[1][l]) * RANK_SCALE, 0.0, 1.0)
            l0 = hit if l0 is None else l0 + hit
        p0 = jnp.exp(scores[0] - tops[0][0])
        for grp in range(l0_ref.shape[0]):
            rows = slice(grp * SUBLANES, (grp + 1) * SUBLANES)
            l0_ref[grp, h] = l0[rows, :]
            p0_ref[grp, h] = p0[rows, :]
        r1_ref[h] = ranks[1].astype(BF16)
        p1z_ref[h] = (jnp.exp(scores[1] - tops[1][0]) / zsum).astype(BF16)
        return carry

    assert n_cand <= cand_scr.shape[0]
    lax.fori_loop(0, nh, head, 0)


def _peer_topk(ht, wqt, keys, tb=256):
    d, t = ht.shape
    nh, nk, kd = PEER_HEADS, PEER_KEYS, PEER_HALF
    n_cand = sum((PEER_TOPK + 1) // (k + 1) for k in range(PEER_TOPK + 1))
    cand_rows = -(-n_cand // SUBLANES) * SUBLANES
    second_spec = pl.BlockSpec((nh, nk, tb), lambda i: (0, 0, i))
    second_shape = jax.ShapeDtypeStruct((nh, nk, t), BF16)
    first_spec = pl.BlockSpec((nk // SUBLANES, nh, SUBLANES, tb), lambda i: (0, 0, 0, i))
    first_shape = jax.ShapeDtypeStruct((nk // SUBLANES, nh, SUBLANES, t), F32)
    return pl.pallas_call(
        _peer_topk_kernel,
        grid=(t // tb,),
        in_specs=[pl.BlockSpec((d, tb), lambda i: (0, i)),
                  pl.BlockSpec((nh * 2 * kd, d), lambda i: (0, 0)),
                  pl.BlockSpec((nh, 2, nk, kd), lambda i: (0, 0, 0, 0))],
        out_specs=[first_spec, first_spec, second_spec, second_spec],
        out_shape=[first_shape, first_shape, second_shape, second_shape],
        scratch_shapes=[pltpu.VMEM((nh * 2 * kd, tb), BF16),
                        pltpu.VMEM((cand_rows, tb), F32)],
        compiler_params=_params("arbitrary"),
        name="peer_topk",
    )(ht, wqt, keys)


def _peer_dense_kernel(ht_ref, u_ref, vt_ref, l0_ref, p0_ref, r1_ref, p1z_ref, x_ref, gate_ref, *rest,
                       ni, tb, n_blocks, last_layer):
    n_post = 1 if last_layer else 3
    post_refs = rest[:n_post]
    o_ref = rest[n_post]
    h_ref = None if last_layer else rest[n_post + 1]
    acc_ref, g0_scr, g1_scr, a_scr = rest[-4:]
    nh, nk = PEER_HEADS, PEER_KEYS
    d = acc_ref.shape[0]
    e = pl.program_id(1)
    pair = 2 * nk
    n_chunks = ni * nk // pair
    acc_rows = d // n_chunks

    @pl.when(e == 0)
    def _():
        acc_ref[...] = jnp.zeros(acc_ref.shape, F32)
        g1_scr[...] = jnp.zeros(g1_scr.shape, BF16)

    def gate_chunk(c, g_cur, parity):
        for sub in range(pair // nk):
            ii = c * (pair // nk) + sub
            row = (parity * ni + ii) % SUBLANES
            for cc in range(tb // LANES):
                cs = slice(cc * LANES, (cc + 1) * LANES)
                for j0 in range(0, nk, GATE_ROWS):
                    js = slice(j0, j0 + GATE_ROWS)
                    w = None
                    for h in range(nh):
                        l0 = l0_ref[0, h, row:row + 1, cs].astype(BF16)
                        p0 = p0_ref[0, h, row:row + 1, cs].astype(BF16)
                        hit = jnp.clip(l0 - r1_ref[h, js, cs], 0.0, 1.0)
                        term = hit * p1z_ref[h, js, cs] * p0
                        w = term if w is None else w + term
                    a = a_scr[sub * nk + j0:sub * nk + j0 + GATE_ROWS, cs]
                    g_cur[ii * nk + j0:ii * nk + j0 + GATE_ROWS, cs] = _gelu(a).astype(BF16) * w

    def first_product(c):
        a_scr[...] = jnp.dot(u_ref[c * pair:(c + 1) * pair, :], ht_ref[...], preferred_element_type=F32)

    def step(g_cur, g_prev, parity):
        for c in range(n_chunks):
            first_product(c)
            rows = slice(c * acc_rows, (c + 1) * acc_rows)
            acc_ref[rows, :] += jnp.dot(vt_ref[rows, :], g_prev[...], preferred_element_type=F32)
            gate_chunk(c, g_cur, parity)

    @pl.when(jnp.logical_and(e < n_blocks, e % 2 == 0))
    def _():
        step(g0_scr, g1_scr, 0)

    @pl.when(jnp.logical_and(e < n_blocks, e % 2 == 1))
    def _():
        step(g1_scr, g0_scr, 1)

    @pl.when(e == n_blocks)
    def _():
        g_last = g1_scr if n_blocks % 2 == 0 else g0_scr
        total = acc_ref[...] + jnp.dot(vt_ref[...], g_last[...], preferred_element_type=F32)
        x_new = x_ref[...] + gate_ref[0] * total.T
        normed = x_new * lax.rsqrt(jnp.mean(x_new * x_new, axis=-1, keepdims=True) + EPS) * post_refs[0][...]
        if last_layer:
            o_ref[...] = normed
        else:
            o_ref[...] = x_new
            h_ref[...] = (normed * (1.0 + post_refs[2][0]) + post_refs[1][0]).astype(BF16)


def _peer_dense(ht, u, vt, l0, p0, r1, p1z, x2, gate, post, seq, tb=512, eb=512):
    d, t = ht.shape
    last_layer = len(post) == 1
    n_exp = u.shape[0]
    nh, nk = PEER_HEADS, PEER_KEYS
    tb, eb = min(tb, seq), min(eb, n_exp)
    ni = eb // nk
    assert ni in (SUBLANES // 2, SUBLANES)
    n_blocks = n_exp // eb
    per_batch = seq // tb
    blk = lambda b: jnp.clip(b, 0, n_blocks - 1)
    score_spec = pl.BlockSpec((nh, nk, tb), lambda i, e: (0, 0, i))
    first_spec = pl.BlockSpec((1, nh, SUBLANES, tb), lambda i, e: (blk(e) * ni // SUBLANES, 0, 0, i))
    batch_spec = pl.BlockSpec((1, 1, d), lambda i, e: (i // per_batch, 0, 0))
    row_spec = pl.BlockSpec((tb, d), lambda i, e: (i, 0))
    post_specs = [pl.BlockSpec((1, d), lambda i, e: (0, 0))] + [batch_spec] * (len(post) - 1)
    post_args = [post[0].reshape(1, d)] + list(post[1:])
    out_specs = [row_spec] if last_layer else [row_spec, row_spec]
    out_shape = [jax.ShapeDtypeStruct((t, d), F32)] + ([] if last_layer else [jax.ShapeDtypeStruct((t, d), BF16)])
    return pl.pallas_call(
        functools.partial(_peer_dense_kernel, ni=ni, tb=tb, n_blocks=n_blocks, last_layer=last_layer),
        grid=(t // tb, n_blocks + 1),
        in_specs=[pl.BlockSpec((d, tb), lambda i, e: (0, i)),
                  pl.BlockSpec((eb, d), lambda i, e: (blk(e), 0)),
                  pl.BlockSpec((d, eb), lambda i, e: (0, blk(e - 1))),
                  first_spec, first_spec, score_spec, score_spec,
                  pl.BlockSpec((tb, d), lambda i, e: (i, 0), pipeline_mode=pl.Buffered(1)),
                  batch_spec] + post_specs,
        out_specs=out_specs,
        out_shape=out_shape,
        scratch_shapes=[pltpu.VMEM((d, tb), F32),
                        pltpu.VMEM((eb, tb), BF16),
                        pltpu.VMEM((eb, tb), BF16),
                        pltpu.VMEM((2 * nk, tb), F32)],
        compiler_params=_params("arbitrary", "arbitrary"),
        name="peer_dense",
    )(ht, u, vt, l0, p0, r1, p1z, x2, gate, *post_args)


def _cast_t_kernel(w_ref, o_ref):
    o_ref[...] = w_ref[0].T.astype(BF16)


def _cast_t(w, layer, tm=512):
    _, m, n = w.shape
    return pl.pallas_call(
        _cast_t_kernel,
        grid=(m // tm,),
        in_specs=[pl.BlockSpec((1, tm, n), lambda i: (layer, i, 0))],
        out_specs=pl.BlockSpec((n, tm), lambda i: (0, i)),
        out_shape=jax.ShapeDtypeStruct((n, m), BF16),
        compiler_params=_params("arbitrary"),
        name="cast_t",
    )(w)


def _peer(x2, g, shift, scale, gate, wq_t, keys, u, v_t, post, seq):
    ht = _normmod(x2, g, shift, scale, seq, transposed=True)
    l0, p0, r1, p1z = _peer_topk(ht, wq_t, keys.astype(BF16))
    return _peer_dense(ht, u.astype(BF16), v_t, l0, p0, r1, p1z, x2, gate, post, seq)


def kernel(x, c, ada_w, ada_b, norm_mix_g, norm_ffn_g, in0_w, conv_w, conv_b, dt_bias, a_log, d_skip,
           ssd_norm_g, gmlp_ln_g, gmlp_ln_b, gmlp_ws, gmlp_bs, out0_w, sb_qkv_w, sb_out_w, peer_wq,
           peer_keys, peer_u, peer_v, final_g):
    bsz, seq, d = x.shape
    depth = ada_w.shape[0]
    xi = SSD_HEADS * SSD_HEAD_DIM
    conv_dim = xi + 2 * SSD_GROUPS * SSD_STATE
    mod = _ada_mod(c, ada_w, ada_b)
    x2 = x.reshape(bsz * seq, d)
    mods = [[m.reshape(bsz, 1, d) for m in jnp.split(mod[i], 6, axis=-1)] for i in range(depth)]
    h = _normmod(x2, norm_mix_g[0], mods[0][0], mods[0][1], seq, transposed=False)
    for i in range(depth):
        shift1, scale1, gate1, shift2, scale2, gate2 = mods[i]
        j = i // 2
        if i % 2 == 0:
            w = in0_w[j]
            dt_cols = slice(xi + conv_dim, xi + conv_dim + SSD_HEADS)
            w_main = jnp.concatenate([w[:, :xi + conv_dim], w[:, dt_cols.stop:]], axis=1).astype(BF16)
            w_dt = jnp.pad(w[:, dt_cols], ((0, 0), (0, LANES - SSD_HEADS))).astype(BF16)
            proj = _matmul([h], [w_main], BF16, 1024, 512, name="in0_proj")
            dt_raw = _matmul([h], [w_dt], F32, 1024, LANES, name="in0_dt")
            ya = _ssd(proj, dt_raw, conv_w[j], conv_b[j], dt_bias[j], a_log[j], d_skip[j], ssd_norm_g[j], seq)
            yb = _gmlp(proj, gmlp_ln_g[j], gmlp_ln_b[j], gmlp_ws[j], gmlp_bs[j])
            wo = out0_w[j].astype(BF16)
            x2 = _matmul([ya, yb], [wo[:xi], wo[xi:]], F32, 1024, 512, resid=x2, gate=gate1, seq=seq,
                         name="out0_proj")
        else:
            w_qkv = sb_qkv_w[j]
            n_q = SB_HEADS * SB_HEAD_DIM
            w_qkv = jnp.concatenate([w_qkv[:, :n_q] * SB_HEAD_DIM ** -0.5, w_qkv[:, n_q:]], axis=1).astype(BF16)
            qkv = _matmul([h], [w_qkv], BF16, 1024, 512, name="qkv_proj")
            o = _attn(qkv, seq)
            x2 = _matmul([o], [sb_out_w[j].astype(BF16)], F32, 1024, 512, resid=x2, gate=gate1, seq=seq,
                         name="sb_out_proj")
        last = i + 1 == depth
        post = (final_g,) if last else (norm_mix_g[i + 1], mods[i + 1][0], mods[i + 1][1])
        res = _peer(x2, norm_ffn_g[i], shift2, scale2, gate2, _cast_t(peer_wq, i), peer_keys[i], peer_u[i],
                    _cast_t(peer_v, i), post, seq)
        if last:
            return res[0].reshape(bsz, seq, d)
        x2, h = res
```
